```python
import math
import jax, jax.numpy as jnp
from jax import lax
import numpy as np

D_MODEL = 1024
BATCH = 1
SEQ = 16384
DEPTH = 2
DEC_BATCH = 32
DEC_SEQ = 1
PAST_LEN = 16384
PAGE_SIZE = 128

SSD_EXPAND = 2
D_INNER = SSD_EXPAND * D_MODEL
SSD_HEAD_DIM = 64
SSD_HEADS = D_INNER // SSD_HEAD_DIM
SSD_GROUPS = 8
SSD_STATE = 128
SSD_CONV = 4
SSD_CHUNK = 128
GN = SSD_GROUPS * SSD_STATE
CONV_DIM = D_INNER + 2 * GN
SSD_IN = D_INNER + CONV_DIM + SSD_HEADS

ATT_HEADS = 16
ATT_HEAD_DIM = 64
ATT_WIDTH = ATT_HEADS * ATT_HEAD_DIM
ATT_IN = 4 * ATT_WIDTH
MOBA_BLOCK = 256
MOBA_TOPK = 3
Q_BLOCK = 128

N_SSD = (DEPTH + 1) // 2
N_ATT = DEPTH // 2
EPS = 1e-6

kernel_name = 'hybrid_ssd_moba_step'

F32 = jnp.float32


def _rmsnorm(x, w):
    xf = x.astype(F32)
    y = xf * lax.rsqrt(jnp.mean(xf * xf, axis=-1, keepdims=True) + EPS)
    return (y * w.astype(F32)).astype(x.dtype)


def _adaln(x, c, ada_w, ada_b, norm_w):
    mod = jax.nn.silu(c) @ ada_w + ada_b
    shift, scale, gate = jnp.split(mod[:, None, :], 3, axis=-1)
    h = _rmsnorm(x, norm_w) * (1 + scale) + shift
    return h, gate


def _alibi_slopes():
    return jnp.asarray(2.0 ** (-8.0 * np.arange(1, ATT_HEADS + 1) / ATT_HEADS), dtype=F32)


def _ssd_inputs(h, w_in, conv_w, conv_b, dt_bias, a_log, conv_state):
    bsz, t, _ = h.shape
    proj = h @ w_in
    z = proj[..., :D_INNER]
    xbc = proj[..., D_INNER:D_INNER + CONV_DIM]
    dt_raw = proj[..., D_INNER + CONV_DIM:]
    xpad = jnp.concatenate([conv_state.astype(xbc.dtype), xbc], axis=1)
    new_conv = xpad[:, t:]
    conv = conv_b
    for j in range(SSD_CONV):
        conv = conv + xpad[:, j:j + t] * conv_w[j]
    xc = jax.nn.silu(conv)
    x = xc[..., :D_INNER].reshape(bsz, t, SSD_HEADS, SSD_HEAD_DIM)
    bm = xc[..., D_INNER:D_INNER + GN].reshape(bsz, t, SSD_GROUPS, SSD_STATE)
    cm = xc[..., D_INNER + GN:].reshape(bsz, t, SSD_GROUPS, SSD_STATE)
    dt = jax.nn.softplus(dt_raw.astype(F32) + dt_bias.astype(F32))
    a = -jnp.exp(a_log.astype(F32))
    return z, x, bm, cm, dt, a, new_conv


def _ssd_chunked(x, dt, a, bm, cm):
    bsz, t = x.shape[:2]
    nc, L, G, R = t // SSD_CHUNK, SSD_CHUNK, SSD_GROUPS, SSD_HEADS // SSD_GROUPS
    xdt = (x.astype(F32) * dt[..., None]).reshape(bsz, nc, L, G, R, SSD_HEAD_DIM)
    acs = jnp.cumsum((dt * a).reshape(bsz, nc, L, G, R), axis=2)
    bc = bm.astype(F32).reshape(bsz, nc, L, G, SSD_STATE)
    cc = cm.astype(F32).reshape(bsz, nc, L, G, SSD_STATE)
    causal = jnp.tril(jnp.ones((L, L), bool))[None, None, :, :, None, None]
    decay = jnp.exp(jnp.where(causal, acs[:, :, :, None] - acs[:, :, None, :], -jnp.inf))
    cb = jnp.einsum('bclgn,bcsgn->bclsg', cc, bc)
    y_diag = jnp.einsum('bclsg,bclsgr,bcsgrp->bclgrp', cb, decay, xdt)
    decay_end = jnp.exp(acs[:, :, -1:] - acs)
    states = jnp.einsum('bclgn,bclgr,bclgrp->bcgrpn', bc, decay_end, xdt)
    chunk_decay = jnp.exp(acs[:, :, -1])

    def step(hc, inp):
        s, dec = inp
        return hc * dec[..., None, None] + s, hc

    h0 = jnp.zeros((bsz, G, R, SSD_HEAD_DIM, SSD_STATE), F32)
    h_fin, h_prev = lax.scan(step, h0, (jnp.moveaxis(states, 1, 0), jnp.moveaxis(chunk_decay, 1, 0)))
    h_prev = jnp.moveaxis(h_prev, 0, 1)
    y_off = jnp.einsum('bclgn,bcgrpn,bclgr->bclgrp', cc, h_prev, jnp.exp(acs))
    y = (y_diag + y_off).reshape(bsz, t, SSD_HEADS, SSD_HEAD_DIM)
    return y, h_fin.reshape(bsz, SSD_HEADS, SSD_HEAD_DIM, SSD_STATE)


def _ssd_recurrent(x, dt, a, bm, cm, h0):
    rep = SSD_HEADS // SSD_GROUPS
    bh = jnp.repeat(bm.astype(F32), rep, axis=2)
    ch = jnp.repeat(cm.astype(F32), rep, axis=2)
    xdt = x.astype(F32) * dt[..., None]

    def step(hc, inp):
        xdt_t, dt_t, b_t, c_t = inp
        hc = hc * jnp.exp(dt_t * a)[..., None, None] + xdt_t[..., None] * b_t[:, :, None, :]
        return hc, jnp.einsum('bhpn,bhn->bhp', hc, c_t)

    h_fin, ys = lax.scan(step, h0.astype(F32), (jnp.moveaxis(xdt, 1, 0), jnp.moveaxis(dt, 1, 0),
                                                  jnp.moveaxis(bh, 1, 0), jnp.moveaxis(ch, 1, 0)))
    return jnp.moveaxis(ys, 0, 1), h_fin


def _ssd_out(y, x, z, d_skip, norm_w, w_out):
    bsz, t = y.shape[:2]
    y = y + x.astype(F32) * d_skip.astype(F32)[:, None]
    g = y.reshape(bsz, t, D_INNER) * jax.nn.silu(z.astype(F32))
    g = g.reshape(bsz, t, SSD_GROUPS, D_INNER // SSD_GROUPS)
    g = g * lax.rsqrt(jnp.mean(g * g, axis=-1, keepdims=True) + EPS)
    g = g.reshape(bsz, t, D_INNER) * norm_w.astype(F32)
    return g.astype(z.dtype) @ w_out


def _to_blocks(a):
    n_len = a.shape[0]
    n_blk = -(-n_len // MOBA_BLOCK)
    a = jnp.pad(a, ((0, n_blk * MOBA_BLOCK - n_len), (0, 0), (0, 0)))
    return jnp.transpose(a.reshape(n_blk, MOBA_BLOCK, ATT_HEADS, ATT_HEAD_DIM), (2, 0, 1, 3))


def _moba_core(q, kbh, vbh, kmean, q_pos, slopes):
    tq = q.shape[0]
    n_blk = kbh.shape[1]
    qf = q.astype(F32)
    cur = q_pos // MOBA_BLOCK
    gate = jnp.einsum('qhd,hnd->qhn', qf, kmean)
    fully_past = jnp.arange(n_blk)[None, None, :] < cur[:, None, None]
    gate = jnp.where(fully_past, gate, -jnp.inf)
    if n_blk < MOBA_TOPK:
        gate = jnp.pad(gate, ((0, 0), (0, 0), (0, MOBA_TOPK - n_blk)), constant_values=-jnp.inf)
    _, top_idx = lax.top_k(gate, MOBA_TOPK)
    valid = top_idx < cur[:, None, None]
    own = jnp.broadcast_to(cur[:, None, None], (tq, ATT_HEADS, 1)).astype(top_idx.dtype)
    blk_idx = jnp.concatenate([jnp.minimum(top_idx, n_blk - 1), own], axis=-1)
    blk_ok = jnp.concatenate([valid, jnp.ones(own.shape, bool)], axis=-1)
    h_idx = jnp.arange(ATT_HEADS)[None, :, None]
    k_sel = kbh[h_idx, blk_idx].astype(F32)
    v_sel = vbh[h_idx, blk_idx].astype(F32)
    k_pos = blk_idx[..., None] * MOBA_BLOCK + jnp.arange(MOBA_BLOCK)
    dist = q_pos[:, None, None, None] - k_pos
    allowed = blk_ok[..., None] & (dist >= 0)
    s = (jnp.einsum('qhd,qhjkd->qhjk', qf, k_sel) * (ATT_HEAD_DIM ** -0.5)
         - slopes[None, :, None, None] * dist.astype(F32))
    s = jnp.where(allowed, s, -jnp.inf).reshape(tq, ATT_HEADS, -1)
    p = jax.nn.softmax(s, axis=-1).reshape(k_pos.shape)
    return jnp.einsum('qhjk,qhjkd->qhd', p, v_sel).astype(q.dtype)


def _moba_prompt(q, k, v, slopes):
    bsz, t = q.shape[:2]
    kbh = jax.vmap(_to_blocks)(k)
    vbh = jax.vmap(_to_blocks)(v)
    kmean = jnp.mean(kbh.astype(F32), axis=3)
    nqb = t // Q_BLOCK
    qb = jnp.moveaxis(q.reshape(bsz, nqb, Q_BLOCK, ATT_HEADS, ATT_HEAD_DIM), 1, 0)
    pos = jnp.arange(t, dtype=jnp.int32).reshape(nqb, Q_BLOCK)
    core = jax.vmap(_moba_core, in_axes=(0, 0, 0, 0, None, None))
    out = lax.map(lambda a: core(a[0], kbh, vbh, kmean, a[1], slopes), (qb, pos))
    return jnp.moveaxis(out, 0, 1).reshape(bsz, t, ATT_HEADS, ATT_HEAD_DIM)


def _moba_sample(q, k, v, pool_k, pool_v, page_table, slopes):
    t = q.shape[1]
    past_len = page_table.shape[1] * pool_k.shape[1]
    pos = past_len + jnp.arange(t, dtype=jnp.int32)

    def one(a):
        q_b, k_b, v_b, pt = a
        k_all = jnp.concatenate([pool_k[pt].reshape(past_len, ATT_HEADS, ATT_HEAD_DIM), k_b.astype(pool_k.dtype)], axis=0)
        v_all = jnp.concatenate([pool_v[pt].reshape(past_len, ATT_HEADS, ATT_HEAD_DIM), v_b.astype(pool_v.dtype)], axis=0)
        kbh = _to_blocks(k_all)
        vbh = _to_blocks(v_all)
        kmean = jnp.mean(kbh.astype(F32), axis=2)
        return _moba_core(q_b, kbh, vbh, kmean, pos, slopes)

    return lax.map(one, (q, k, v, page_table))


def _att_project(h, w_in):
    bsz, t, _ = h.shape
    proj = h @ w_in
    q, k, v, g = jnp.split(proj, 4, axis=-1)
    hs = (bsz, t, ATT_HEADS, ATT_HEAD_DIM)
    return q.reshape(hs), k.reshape(hs), v.reshape(hs), g


def _att_out(o, g, w_out):
    bsz, t = o.shape[:2]
    return (o.reshape(bsz, t, ATT_WIDTH) * jax.nn.silu(g)) @ w_out


def setup_inputs(seed: int = 0) -> dict:
    key = jax.random.key(seed)
    ks = jax.random.split(key, 24)
    n_pages = PAST_LEN // PAGE_SIZE
    n_used = DEC_BATCH * n_pages
    n_phys = (5 * n_used + 3) // 4

    def nrm(k, shape, s):
        return jax.random.normal(k, shape, F32) * s

    dt0 = jnp.exp(jax.random.uniform(ks[15], (N_SSD, SSD_HEADS), F32, math.log(1e-3), math.log(1e-1)))
    return {
        'x_prompt': nrm(ks[0], (BATCH, SEQ, D_MODEL), 1.0),
        'x_sample': nrm(ks[1], (DEC_BATCH, DEC_SEQ, D_MODEL), 1.0),
        'cache_k': nrm(ks[2], (N_ATT, n_phys, PAGE_SIZE, ATT_HEADS, ATT_HEAD_DIM), 1.0),
        'cache_v': nrm(ks[3], (N_ATT, n_phys, PAGE_SIZE, ATT_HEADS, ATT_HEAD_DIM), 1.0),
        'state_ssm': nrm(ks[4], (N_SSD, DEC_BATCH, SSD_HEADS, SSD_HEAD_DIM, SSD_STATE), 0.1),
        'state_conv': nrm(ks[5], (N_SSD, DEC_BATCH, SSD_CONV - 1, CONV_DIM), 1.0),
        'page_table': jax.random.permutation(ks[6], n_phys)[:n_used].reshape(DEC_BATCH, n_pages).astype(jnp.int32),
        'c_prompt': nrm(ks[7], (BATCH, D_MODEL), 1.0),
        'c_sample': nrm(ks[8], (DEC_BATCH, D_MODEL), 1.0),
        'ada_w': nrm(ks[9], (DEPTH, D_MODEL, 3 * D_MODEL), 0.5 * D_MODEL ** -0.5),
        'ada_b': nrm(ks[10], (DEPTH, 3 * D_MODEL), 0.02),
        'norm_w': 1.0 + nrm(ks[11], (DEPTH, D_MODEL), 0.02),
        'ssd_w_in': nrm(ks[12], (N_SSD, D_MODEL, SSD_IN), D_MODEL ** -0.5),
        'ssd_conv_w': nrm(ks[13], (N_SSD, SSD_CONV, CONV_DIM), SSD_CONV ** -0.5),
        'ssd_conv_b': nrm(ks[14], (N_SSD, CONV_DIM), 0.02),
        'ssd_dt_bias': dt0 + jnp.log(-jnp.expm1(-dt0)),
        'ssd_a_log': jnp.log(jax.random.uniform(ks[16], (N_SSD, SSD_HEADS), F32, 1.0, 16.0)),
        'ssd_d': 1.0 + nrm(ks[17], (N_SSD, SSD_HEADS), 0.02),
        'ssd_norm_w': 1.0 + nrm(ks[18], (N_SSD, D_INNER), 0.02),
        'ssd_w_out': nrm(ks[19], (N_SSD, D_INNER, D_MODEL), D_INNER ** -0.5),
        'att_w_in': nrm(ks[20], (N_ATT, D_MODEL, ATT_IN), D_MODEL ** -0.5),
        'att_w_out': nrm(ks[21], (N_ATT, ATT_WIDTH, D_MODEL), ATT_WIDTH ** -0.5),
        'final_norm_w': 1.0 + nrm(ks[22], (D_MODEL,), 0.02),
    }


def reference(x_prompt, x_sample, cache_k, cache_v, state_ssm, state_conv, page_table, c_prompt, c_sample,
              ada_w, ada_b, norm_w, ssd_w_in, ssd_conv_w, ssd_conv_b, ssd_dt_bias, ssd_a_log, ssd_d,
              ssd_norm_w, ssd_w_out, att_w_in, att_w_out, final_norm_w):
    slopes = _alibi_slopes()
    xp, xs = x_prompt, x_sample
    kp_l, vp_l, ks_l, vs_l = [], [], [], []
    ssm_p_l, conv_p_l, ssm_s_l, conv_s_l = [], [], [], []
    for i in range(DEPTH):
        j = i // 2
        hp, gp = _adaln(xp, c_prompt, ada_w[i], ada_b[i], norm_w[i])
        hs, gs = _adaln(xs, c_sample, ada_w[i], ada_b[i], norm_w[i])
        if i % 2 == 0:
            conv0 = jnp.zeros((hp.shape[0], SSD_CONV - 1, CONV_DIM), hp.dtype)
            z, x, bm, cm, dt, a, cnew_p = _ssd_inputs(hp, ssd_w_in[j], ssd_conv_w[j], ssd_conv_b[j],
                                                      ssd_dt_bias[j], ssd_a_log[j], conv0)
            y, h_p = _ssd_chunked(x, dt, a, bm, cm)
            op = _ssd_out(y, x, z, ssd_d[j], ssd_norm_w[j], ssd_w_out[j])
            z, x, bm, cm, dt, a, cnew_s = _ssd_inputs(hs, ssd_w_in[j], ssd_conv_w[j], ssd_conv_b[j],
                                                      ssd_dt_bias[j], ssd_a_log[j], state_conv[j])
            y, h_s = _ssd_recurrent(x, dt, a, bm, cm, state_ssm[j])
            os_ = _ssd_out(y, x, z, ssd_d[j], ssd_norm_w[j], ssd_w_out[j])
            ssm_p_l.append(h_p.astype(xp.dtype))
            conv_p_l.append(cnew_p)
            ssm_s_l.append(h_s.astype(xs.dtype))
            conv_s_l.append(cnew_s)
        else:
            q, k, v, g = _att_project(hp, att_w_in[j])
            op = _att_out(_moba_prompt(q, k, v, slopes), g, att_w_out[j])
            bsz, t = k.shape[:2]
            kp_l.append(k.reshape(bsz, t // PAGE_SIZE, PAGE_SIZE, ATT_HEADS, ATT_HEAD_DIM))
            vp_l.append(v.reshape(bsz, t // PAGE_SIZE, PAGE_SIZE, ATT_HEADS, ATT_HEAD_DIM))
            q, k, v, g = _att_project(hs, att_w_in[j])
            os_ = _att_out(_moba_sample(q, k, v, cache_k[j], cache_v[j], page_table, slopes), g, att_w_out[j])
            ks_l.append(k)
            vs_l.append(v)
        xp = xp + gp * op
        xs = xs + gs * os_
    y_prompt = _rmsnorm(xp, final_norm_w)
    y_sample = _rmsnorm(xs, final_norm_w)
    return (y_prompt, y_sample, jnp.stack(kp_l), jnp.stack(vp_l), jnp.stack(ks_l), jnp.stack(vs_l),
            jnp.stack(ssm_p_l), jnp.stack(conv_p_l), jnp.stack(ssm_s_l), jnp.stack(conv_s_l))
```

```python
import functools

import numpy as np
import jax
import jax.numpy as jnp
from jax import lax
from jax.experimental import pallas as pl
from jax.experimental.pallas import tpu as pltpu

F32 = jnp.float32
BF16 = jnp.bfloat16
HI = lax.Precision.HIGHEST

D_MODEL = 1024
D_INNER = 2048
SSD_HEADS = 32
SSD_HEAD_DIM = 64
SSD_GROUPS = 8
SSD_STATE = 128
SSD_CONV = 4
CHUNK = 128
GN = SSD_GROUPS * SSD_STATE
CONV_DIM = D_INNER + 2 * GN
ATT_HEADS = 16
ATT_HEAD_DIM = 64
ATT_WIDTH = ATT_HEADS * ATT_HEAD_DIM
MOBA_BLOCK = 256
MOBA_TOPK = 3
PAGE = 128
EPS = 1e-6
NEG = -1e30

VMEM_LIMIT = 56 * 1024 * 1024

NT = (((1,), (1,)), ((), ()))
TN = (((0,), (0,)), ((), ()))


def _silu(x):
    return x / (1.0 + jnp.exp(-x))


def _softplus(x):
    return jnp.maximum(x, 0.0) + jnp.log1p(jnp.exp(-jnp.abs(x)))


def _split3(x):
    hi = x.astype(BF16)
    r1 = x - hi.astype(F32)
    mid = r1.astype(BF16)
    lo = (r1 - mid.astype(F32)).astype(BF16)
    return hi, mid, lo


def _expand_lanes(x, e_ref):
    hi, mid, lo = _split3(x)
    e = e_ref[...]
    out = jnp.dot(hi, e, preferred_element_type=F32)
    out = out + jnp.dot(mid, e, preferred_element_type=F32)
    return out + jnp.dot(lo, e, preferred_element_type=F32)


def _params(sem):
    return pltpu.CompilerParams(dimension_semantics=sem, vmem_limit_bytes=VMEM_LIMIT)


def _ada_kernel(c_ref, w_ref, b_ref, o_ref):
    sc = _silu(c_ref[...])
    o_ref[...] = jnp.dot(sc, w_ref[...], precision=HI, preferred_element_type=F32) + b_ref[...]


def _ada_mod(c_rows, ada_w, ada_b):
    depth, d, n3 = ada_w.shape
    m = c_rows.shape[0]
    tn = 1024
    return pl.pallas_call(
        _ada_kernel,
        grid=(depth, n3 // tn),
        in_specs=[
            pl.BlockSpec((m, d), lambda i, j: (0, 0)),
            pl.BlockSpec((None, d, tn), lambda i, j: (i, 0, j)),
            pl.BlockSpec((None, 1, tn), lambda i, j: (i, 0, j)),
        ],
        out_specs=pl.BlockSpec((None, m, tn), lambda i, j: (i, 0, j)),
        out_shape=jax.ShapeDtypeStruct((depth, m, n3), F32),
        compiler_params=_params(("arbitrary", "arbitrary")),
        name="ada_mod",
    )(c_rows, ada_w, ada_b.reshape(depth, 1, n3))


def _norm_mod(x, nw, scale, shift):
    ms = jnp.mean(x * x, axis=-1, keepdims=True)
    h = x * lax.rsqrt(ms + EPS) * nw
    return h * (1.0 + scale) + shift


def _norm_linear_kernel(x_ref, nw_ref, sc_ref, sh_ref, w_ref, o_ref, *extra, copy):
    h = _norm_mod(x_ref[...], nw_ref[...], sc_ref[...], sh_ref[...])
    acc = jnp.dot(h.astype(BF16), w_ref[...], preferred_element_type=F32)
    o_ref[...] = acc
    if copy == "bf16":
        extra[0][...] = acc.astype(BF16)
    elif copy == "bf16_t":
        extra[0][...] = acc.T.astype(BF16)


def _norm_linear(x, nw, scale, shift, w_bf16, *, tm, tn, copy=None):
    m, k = x.shape
    n = w_bf16.shape[1]
    per_row = scale.shape[0] != 1
    vec = (pl.BlockSpec((tm, k), lambda j, i: (i, 0)) if per_row
           else pl.BlockSpec((1, k), lambda j, i: (0, 0)))
    out_shape = [jax.ShapeDtypeStruct((m, n), F32)]
    out_specs = [pl.BlockSpec((tm, tn), lambda j, i: (i, j))]
    if copy == "bf16":
        out_shape.append(jax.ShapeDtypeStruct((m, n), BF16))
        out_specs.append(pl.BlockSpec((tm, tn), lambda j, i: (i, j)))
    elif copy == "bf16_t":
        out_shape.append(jax.ShapeDtypeStruct((m // tm, n, tm), BF16))
        out_specs.append(pl.BlockSpec((None, tn, tm), lambda j, i: (i, j, 0)))
    res = pl.pallas_call(
        functools.partial(_norm_linear_kernel, copy=copy),
        grid=(n // tn, m // tm),
        in_specs=[
            pl.BlockSpec((tm, k), lambda j, i: (i, 0)),
            pl.BlockSpec((1, k), lambda j, i: (0, 0)),
            vec, vec,
            pl.BlockSpec((k, tn), lambda j, i: (0, j)),
        ],
        out_specs=out_specs,
        out_shape=out_shape,
        compiler_params=_params(("arbitrary", "arbitrary")),
        name="norm_linear",
    )(x, nw, scale, shift, w_bf16)
    return res if copy else res[0]


def _gated_out_kernel(*refs, has_g, has_final):
    it = iter(refs)
    a_ref = next(it)
    g_ref = next(it) if has_g else None
    x_ref, gate_ref, w_ref = next(it), next(it), next(it)
    fw_ref = next(it) if has_final else None
    o_ref = next(it)
    a = a_ref[...].astype(F32)
    if has_g:
        a = a * _silu(g_ref[...])
    y = x_ref[...] + gate_ref[...] * jnp.dot(a.astype(BF16), w_ref[...], preferred_element_type=F32)
    if has_final:
        ms = jnp.mean(y * y, axis=-1, keepdims=True)
        y = y * lax.rsqrt(ms + EPS) * fw_ref[...]
    o_ref[...] = y


def _gated_out(a, g, x, gate, w_bf16, final_w, *, tm):
    m, k = a.shape
    n = w_bf16.shape[1]
    per_row = gate.shape[0] != 1
    row = lambda width: pl.BlockSpec((tm, width), lambda i: (i, 0))
    one = lambda width: pl.BlockSpec((1, width), lambda i: (0, 0))
    ins, specs = [a], [row(k)]
    if g is not None:
        ins.append(g)
        specs.append(row(k))
    ins += [x, gate, w_bf16]
    specs += [row(n), row(n) if per_row else one(n), pl.BlockSpec((k, n), lambda i: (0, 0))]
    if final_w is not None:
        ins.append(final_w)
        specs.append(one(n))
    return pl.pallas_call(
        functools.partial(_gated_out_kernel, has_g=g is not None, has_final=final_w is not None),
        grid=(m // tm,),
        in_specs=specs,
        out_specs=row(n),
        out_shape=jax.ShapeDtypeStruct((m, n), F32),
        compiler_params=_params(("arbitrary",)),
        name="gated_out",
    )(*ins)


def _ssd_prompt_kernel(x_ref, xbc_ref, z_ref, modv_ref, wdt_ref, wdtT_ref, hrow_ref, hcol_ref,
                       cw_ref, cb_ref, dexp_ref, snw_ref, e64_ref, e128_ref, wout_ref,
                       xo_ref, st_ref, tail_ref,
                       xpad, xg_s, b_s, c_s, xdt_s, abc_s, acst_s, y_s, gn_s):
    L = CHUNK
    ci = pl.program_id(0)
    nci = pl.num_programs(0)

    @pl.when(ci == 0)
    def _():
        xpad[0:8, :] = jnp.zeros((8, CONV_DIM), F32)
        st_ref[...] = jnp.zeros(st_ref.shape, F32)

    x = x_ref[...]
    hn = _norm_mod(x, modv_ref[0:1, :], modv_ref[1:2, :], modv_ref[2:3, :]).astype(BF16)
    dt_raw = jnp.dot(hn, wdt_ref[...], preferred_element_type=F32)
    dt_rawT = lax.dot_general(wdtT_ref[...], hn, NT, preferred_element_type=F32)
    dt = _softplus(dt_raw + hrow_ref[0:1, :])
    dtT = _softplus(dt_rawT + hcol_ref[:, 0:1])
    dA = dt * (-jnp.exp(hrow_ref[1:2, :]))
    dAT = dtT * (-jnp.exp(hcol_ref[:, 1:2]))
    r_i = lax.broadcasted_iota(jnp.int32, (L, L), 0)
    c_i = lax.broadcasted_iota(jnp.int32, (L, L), 1)
    tril = (r_i >= c_i).astype(F32)
    triu = (r_i <= c_i).astype(F32)
    acs = jnp.dot(tril, dA, precision=HI, preferred_element_type=F32)
    acsT = jnp.dot(dAT, triu, precision=HI, preferred_element_type=F32)
    for h in range(SSD_HEADS):
        acst_s[h] = acsT[h:h + 1, :]
    abig = _expand_lanes(acs, e128_ref)
    for h in range(SSD_HEADS):
        abc_s[h] = abig[:, h * 128:(h + 1) * 128]
    dt_exp = _expand_lanes(dt, e64_ref)

    xpad[8:8 + L, :] = xbc_ref[...]
    cw = 512
    for blk in range(CONV_DIM // cw):
        cs = slice(blk * cw, (blk + 1) * cw)
        acc = cb_ref[:, cs] + xpad[5:5 + L, cs] * cw_ref[0:1, cs]
        acc = acc + xpad[6:6 + L, cs] * cw_ref[1:2, cs]
        acc = acc + xpad[7:7 + L, cs] * cw_ref[2:3, cs]
        acc = acc + xpad[8:8 + L, cs] * cw_ref[3:4, cs]
        xc = _silu(acc)
        if blk < 4:
            for q in range(2):
                g = blk * 2 + q
                xg = xc[:, q * 256:(q + 1) * 256]
                xg_s[g] = xg
                xd = xg * dt_exp[:, g * 256:(g + 1) * 256]
                xdt_s[2 * g] = xd[:, 0:128]
                xdt_s[2 * g + 1] = xd[:, 128:256]
        elif blk < 6:
            for q in range(4):
                b_s[(blk - 4) * 4 + q] = xc[:, q * 128:(q + 1) * 128].astype(BF16)
        else:
            for q in range(4):
                c_s[(blk - 6) * 4 + q] = xc[:, q * 128:(q + 1) * 128]

    @pl.when(ci == nci - 1)
    def _():
        tail_ref[...] = xpad[5 + L:8 + L, :]

    xpad[5:8, :] = xpad[5 + L:8 + L, :]

    causal = r_i >= c_i
    lane = lax.broadcasted_iota(jnp.int32, (L, 128), 1)
    sub = lax.broadcasted_iota(jnp.int32, (128, 128), 0)
    lo_lane = lane < 64
    lo_sub = sub < 64

    def group_body(g, carry):
        bb = b_s[g]
        cg = c_s[g]
        cbm = lax.dot_general(cg.astype(BF16), bb, NT, preferred_element_type=F32)
        for pr in range(2):
            pidx = 2 * g + pr
            xdt_pair = xdt_s[pidx]
            hprev = st_ref[pidx]
            hprev_b = hprev.astype(BF16)
            ypair = jnp.zeros((L, 128), F32)
            dends, cds = [], []
            for hh in range(2):
                h = 2 * pidx + hh
                abc = abc_s[h]
                arow = acst_s[h]
                dec = jnp.exp(jnp.where(causal, abc - arow, NEG))
                mh = (cbm * dec).astype(BF16)
                keep = lo_lane if hh == 0 else jnp.logical_not(lo_lane)
                xm = jnp.where(keep, xdt_pair, 0.0).astype(BF16)
                ypair = ypair + jnp.dot(mh, xm, preferred_element_type=F32)
                cd = (cg * jnp.exp(abc)).astype(BF16)
                yo = lax.dot_general(cd, hprev_b, NT, preferred_element_type=F32)
                ypair = ypair + jnp.where(keep, yo, 0.0)
                tot = abc[L - 1:L, :]
                dends.append(jnp.exp(tot - abc))
                cds.append(jnp.exp(tot))
            y_s[pidx] = ypair
            dend = jnp.where(lo_lane, dends[0], dends[1])
            xw = (xdt_pair * dend).astype(BF16)
            states = lax.dot_general(xw, bb, TN, preferred_element_type=F32)
            cdm = jnp.where(lo_sub, jnp.broadcast_to(cds[0], (128, 128)),
                            jnp.broadcast_to(cds[1], (128, 128)))
            st_ref[pidx] = hprev * cdm + states
        return carry

    lax.fori_loop(0, SSD_GROUPS, group_body, 0)

    for g in range(SSD_GROUPS):
        cs = slice(g * 256, (g + 1) * 256)
        yg = jnp.concatenate([y_s[2 * g], y_s[2 * g + 1]], axis=1)
        gg = (yg + xg_s[g] * dexp_ref[:, cs]) * _silu(z_ref[:, cs])
        ms = jnp.mean(gg * gg, axis=-1, keepdims=True)
        gn_s[:, cs] = (gg * lax.rsqrt(ms + EPS) * snw_ref[:, cs]).astype(BF16)

    xo_ref[...] = x + modv_ref[3:4, :] * jnp.dot(gn_s[...], wout_ref[...], preferred_element_type=F32)


def _ssd_prompt(x, proj, modv, wdt, wdtT, hrow, hcol, conv_w, conv_b, dexp, snw, e64, e128, wout):
    t = x.shape[0]
    nc = t // CHUNK
    L = CHUNK
    full = lambda a: pl.BlockSpec(a.shape, lambda c: (0,) * a.ndim)
    return pl.pallas_call(
        _ssd_prompt_kernel,
        grid=(nc,),
        in_specs=[
            pl.BlockSpec((L, D_MODEL), lambda c: (c, 0)),
            pl.BlockSpec((L, CONV_DIM), lambda c: (c, 0)),
            pl.BlockSpec((L, D_INNER), lambda c: (c, 2)),
            full(modv), full(wdt), full(wdtT), full(hrow), full(hcol), full(conv_w), full(conv_b),
            full(dexp), full(snw), full(e64), full(e128), full(wout),
        ],
        out_specs=[
            pl.BlockSpec((L, D_MODEL), lambda c: (c, 0)),
            pl.BlockSpec((SSD_HEADS // 2, 128, SSD_STATE), lambda c: (0, 0, 0)),
            pl.BlockSpec((SSD_CONV - 1, CONV_DIM), lambda c: (0, 0)),
        ],
        out_shape=[
            jax.ShapeDtypeStruct((t, D_MODEL), F32),
            jax.ShapeDtypeStruct((SSD_HEADS // 2, 128, SSD_STATE), F32),
            jax.ShapeDtypeStruct((SSD_CONV - 1, CONV_DIM), F32),
        ],
        scratch_shapes=[
            pltpu.VMEM((8 + L, CONV_DIM), F32),
            pltpu.VMEM((SSD_GROUPS, L, 256), F32),
            pltpu.VMEM((SSD_GROUPS, L, SSD_STATE), BF16),
            pltpu.VMEM((SSD_GROUPS, L, SSD_STATE), F32),
            pltpu.VMEM((SSD_HEADS // 2, L, 128), F32),
            pltpu.VMEM((SSD_HEADS, L, 128), F32),
            pltpu.VMEM((SSD_HEADS, 1, L), F32),
            pltpu.VMEM((SSD_HEADS // 2, L, 128), F32),
            pltpu.VMEM((L, D_INNER), BF16),
        ],
        compiler_params=_params(("arbitrary",)),
        name="ssd_prompt",
    )(x, proj, proj, modv, wdt, wdtT, hrow, hcol, conv_w, conv_b, dexp, snw, e64, e128, wout)


def _expansion(heads, width):
    e = np.zeros((heads, heads * width), np.float32)
    for h in range(heads):
        e[h, h * width:(h + 1) * width] = 1.0
    return jnp.asarray(e, BF16)


def _split_mod(mod):
    return mod[..., :D_MODEL], mod[..., D_MODEL:2 * D_MODEL], mod[..., 2 * D_MODEL:]


def _prompt_ssd_layer(x, mod, norm_w, w_in, conv_w, conv_b, dt_bias, a_log, d_skip, ssd_norm_w, w_out):
    shift, scale, gate = _split_mod(mod)
    row = lambda v: v.reshape(1, -1)
    w_main = jnp.concatenate([w_in[:, D_INNER:D_INNER + CONV_DIM], w_in[:, :D_INNER]], axis=1).astype(BF16)
    w_dt = w_in[:, D_INNER + CONV_DIM:].astype(BF16)
    proj = _norm_linear(x, row(norm_w), row(scale), row(shift), w_main, tm=512, tn=2048)
    zeros = jnp.zeros((4, D_MODEL), F32)
    modv = jnp.concatenate([row(norm_w), row(scale), row(shift), row(gate), zeros], axis=0)
    hrow = jnp.concatenate([row(dt_bias), row(a_log), jnp.zeros((6, SSD_HEADS), F32)], axis=0)
    hcol = hrow.T
    dexp = jnp.repeat(d_skip, SSD_HEAD_DIM).reshape(1, D_INNER)
    return _ssd_prompt(x, proj, modv, w_dt, w_dt.T, hrow, hcol, conv_w, row(conv_b), dexp,
                       row(ssd_norm_w), _expansion(SSD_HEADS, SSD_HEAD_DIM),
                       _expansion(SSD_HEADS, 128), w_out.astype(BF16))


def _block_mean_kernel(k_ref, o_ref):
    o_ref[...] = jnp.mean(k_ref[...], axis=0, keepdims=True)


def _block_mean(k):
    t, w = k.shape
    nb = t // MOBA_BLOCK
    out = pl.pallas_call(
        _block_mean_kernel,
        grid=(nb,),
        in_specs=[pl.BlockSpec((MOBA_BLOCK, w), lambda j: (j, 0))],
        out_specs=pl.BlockSpec((None, 1, w), lambda j: (j, 0, 0)),
        out_shape=jax.ShapeDtypeStruct((nb, 1, w), F32),
        compiler_params=_params(("arbitrary",)),
        name="block_mean",
    )(k)
    return out.reshape(nb, w)


def _topk_block_mask(gate, blk, n_valid):
    nb = gate.shape[0]
    ninf = jnp.float32(-jnp.inf)
    blk_f = blk.astype(F32)
    g = jnp.where(blk < n_valid, gate, ninf)
    sel = jnp.zeros(gate.shape, jnp.bool_)
    for _ in range(MOBA_TOPK):
        m = jnp.max(g, axis=0, keepdims=True)
        first = jnp.min(jnp.where(g == m, blk_f, float(nb)), axis=0, keepdims=True)
        pick = jnp.logical_and(blk_f == first, m > ninf)
        sel = jnp.logical_or(sel, pick)
        g = jnp.where(pick, ninf, g)
    return jnp.where(sel, 0.0, NEG)


def _moba_select_kernel(q_ref, km_ref, o_ref):
    c = pl.program_id(0)
    nb = km_ref.shape[0]
    tq = q_ref.shape[0]
    lane = lax.broadcasted_iota(jnp.int32, (tq, 128), 1)
    blk = lax.broadcasted_iota(jnp.int32, (nb, tq), 0)
    for p in range(ATT_HEADS // 2):
        qp = q_ref[:, p * 128:(p + 1) * 128]
        kmp = km_ref[:, p * 128:(p + 1) * 128]
        for hh in range(2):
            keep = (lane < 64) if hh == 0 else (lane >= 64)
            qm = jnp.where(keep, qp, 0.0)
            gate = lax.dot_general(kmp, qm, NT, precision=HI, preferred_element_type=F32)
            o_ref[2 * p + hh] = _topk_block_mask(gate, blk, c)


def _moba_select(q, kmean):
    t, w = q.shape
    nb = kmean.shape[0]
    return pl.pallas_call(
        _moba_select_kernel,
        grid=(nb,),
        in_specs=[pl.BlockSpec((MOBA_BLOCK, w), lambda c: (c, 0)),
                  pl.BlockSpec((nb, w), lambda c: (0, 0))],
        out_specs=pl.BlockSpec((ATT_HEADS, nb, MOBA_BLOCK), lambda c: (0, 0, c)),
        out_shape=jax.ShapeDtypeStruct((ATT_HEADS, nb, t), F32),
        compiler_params=_params(("arbitrary",)),
        name="moba_select",
    )(q, kmean)


def _moba_attn_kernel(slopes_ref, q_ref, k_ref, vt_ref, mask_ref, o_ref, acc_s, ml_s):
    B = MOBA_BLOCK
    p = pl.program_id(0)
    c = pl.program_id(1)
    q = q_ref[...] * (ATT_HEAD_DIM ** -0.5)
    lane = lax.broadcasted_iota(jnp.int32, (B, 128), 1)
    qh = [jnp.where(lane < 64, q, 0.0).astype(BF16), jnp.where(lane >= 64, q, 0.0).astype(BF16)]
    s_i = lax.broadcasted_iota(jnp.int32, (B, B), 0)
    t_i = lax.broadcasted_iota(jnp.int32, (B, B), 1)
    dloc = (s_i - t_i).astype(F32)
    sl = [slopes_ref[2 * p], slopes_ref[2 * p + 1]]
    bh = [dloc * sl[0], dloc * sl[1]]
    lo = lax.broadcasted_iota(jnp.int32, (128, B), 0) < 64

    kb = k_ref[c]
    vb = vt_ref[c]
    pvs = []
    for hh in range(2):
        st = lax.dot_general(kb, qh[hh], NT, preferred_element_type=F32)
        sp = jnp.where(s_i <= t_i, st + bh[hh], NEG)
        m = jnp.max(sp, axis=0, keepdims=True)
        pe = jnp.exp(sp - m)
        ml_s[hh:hh + 1, :] = m
        ml_s[2 + hh:3 + hh, :] = jnp.sum(pe, axis=0, keepdims=True)
        pvs.append(jnp.dot(vb, pe.astype(BF16), preferred_element_type=F32))
    acc_s[...] = jnp.where(lo, pvs[0], pvs[1])

    def body(j, carry):
        kb = k_ref[j]
        vb = vt_ref[j]
        far = (c - j).astype(F32) * float(B)
        alphas, pvs = [], []
        for hh in range(2):
            radd = mask_ref[hh, pl.ds(j, 1), :] - far * sl[hh]
            st = lax.dot_general(kb, qh[hh], NT, preferred_element_type=F32)
            sp = st + bh[hh] + radd
            m_old = ml_s[hh:hh + 1, :]
            m_new = jnp.maximum(m_old, jnp.max(sp, axis=0, keepdims=True))
            alpha = jnp.exp(m_old - m_new)
            pe = jnp.exp(sp - m_new)
            ml_s[hh:hh + 1, :] = m_new
            ml_s[2 + hh:3 + hh, :] = alpha * ml_s[2 + hh:3 + hh, :] + jnp.sum(pe, axis=0, keepdims=True)
            alphas.append(alpha)
            pvs.append(jnp.dot(vb, pe.astype(BF16), preferred_element_type=F32))
        acc_s[...] = acc_s[...] * jnp.where(lo, alphas[0], alphas[1]) + jnp.where(lo, pvs[0], pvs[1])
        return carry

    lax.fori_loop(0, c, body, 0)
    linv = jnp.where(lo, 1.0 / ml_s[2:3, :], 1.0 / ml_s[3:4, :])
    o_ref[...] = (acc_s[...] * linv).T


def _moba_attn(q, k_blocks, vt_blocks, mask, slopes):
    t, w = q.shape
    nb = k_blocks.shape[0]
    B = MOBA_BLOCK
    return pl.pallas_call(
        _moba_attn_kernel,
        grid=(ATT_HEADS // 2, nb),
        in_specs=[
            pl.BlockSpec(memory_space=pltpu.SMEM),
            pl.BlockSpec((B, 128), lambda p, c: (c, p)),
            pl.BlockSpec((nb, B, 128), lambda p, c: (0, 0, p)),
            pl.BlockSpec((nb, 128, B), lambda p, c: (0, p, 0)),
            pl.BlockSpec((2, nb, B), lambda p, c: (p, 0, c)),
        ],
        out_specs=pl.BlockSpec((B, 128), lambda p, c: (c, p)),
        out_shape=jax.ShapeDtypeStruct((t, w), F32),
        scratch_shapes=[pltpu.VMEM((128, B), F32), pltpu.VMEM((8, B), F32)],
        compiler_params=_params(("arbitrary", "arbitrary")),
        name="moba_attn",
    )(slopes, q, k_blocks, vt_blocks, mask)


def _alibi_slopes():
    return jnp.asarray(2.0 ** (-8.0 * np.arange(1, ATT_HEADS + 1) / ATT_HEADS), dtype=F32)


def _prompt_moba_layer(x, mod, norm_w, w_in, w_out, final_w):
    shift, scale, gate = _split_mod(mod)
    row = lambda v: v.reshape(1, -1)
    nw, sc, sh = row(norm_w), row(scale), row(shift)
    wq, wk, wv, wg = (w_in[:, i * ATT_WIDTH:(i + 1) * ATT_WIDTH].astype(BF16) for i in range(4))
    B = MOBA_BLOCK
    nb = x.shape[0] // B
    q = _norm_linear(x, nw, sc, sh, wq, tm=512, tn=1024)
    k, k16 = _norm_linear(x, nw, sc, sh, wk, tm=512, tn=1024, copy="bf16")
    v, vt16 = _norm_linear(x, nw, sc, sh, wv, tm=B, tn=1024, copy="bf16_t")
    g = _norm_linear(x, nw, sc, sh, wg, tm=512, tn=1024)
    mask = _moba_select(q, _block_mean(k))
    o = _moba_attn(q, k16.reshape(nb, B, ATT_WIDTH), vt16, mask, _alibi_slopes())
    y = _gated_out(o, g, x, row(gate), w_out.astype(BF16), row(final_w), tm=512)
    return y, k, v


def _ssd_step_kernel(xbc_ref, z_ref, dtr_ref, cst_ref, st_ref, hrow_ref, cw_ref, cb_ref, dexp_ref, snw_ref,
                     e64_ref, sel_ref, gn_ref, sto_ref, cso_ref):
    xbc = xbc_ref[...]
    conv = cb_ref[...] + cst_ref[0:1, :] * cw_ref[0:1, :]
    conv = conv + cst_ref[1:2, :] * cw_ref[1:2, :]
    conv = conv + cst_ref[2:3, :] * cw_ref[2:3, :]
    conv = conv + xbc * cw_ref[3:4, :]
    cso_ref[0:2, :] = cst_ref[1:3, :]
    cso_ref[2:3, :] = xbc
    xc = _silu(conv)
    xs = xc[:, :D_INNER]
    dt = _softplus(dtr_ref[:, :SSD_HEADS] + hrow_ref[0:1, :])
    dec = jnp.exp(dt * (-jnp.exp(hrow_ref[1:2, :])))
    both = _expand_lanes(jnp.concatenate([dt, dec, jnp.zeros((6, SSD_HEADS), F32)], axis=0), e64_ref)
    xdt = xs * both[0:1, :]
    dec_exp = both[1:2, :]
    pieces = [p.astype(F32) for p in _split3(xdt) + _split3(dec_exp)]
    stack = jnp.concatenate(pieces + [jnp.zeros((10, D_INNER), F32)], axis=0).astype(BF16)
    cols = lax.dot_general(stack, sel_ref[...], TN, preferred_element_type=F32)
    ys = []
    for i in range(SSD_HEADS // 2):
        g = i // 2
        brow = xc[:, D_INNER + g * 128:D_INNER + (g + 1) * 128]
        crow = xc[:, D_INNER + GN + g * 128:D_INNER + GN + (g + 1) * 128]
        xcol = cols[i * 128:(i + 1) * 128, 0:1]
        dcol = cols[i * 128:(i + 1) * 128, 1:2]
        hn = st_ref[i] * dcol + xcol * brow
        sto_ref[i] = hn
        c8 = jnp.broadcast_to(crow, (8, SSD_STATE)).astype(BF16)
        ys.append(lax.dot_general(c8, hn.astype(BF16), NT, preferred_element_type=F32)[0:1, :])
    y = jnp.concatenate(ys, axis=1)
    gg = (y + xs * dexp_ref[...]) * _silu(z_ref[...])
    outs = []
    for g in range(SSD_GROUPS):
        gs = gg[:, g * 256:(g + 1) * 256]
        ms = jnp.mean(gs * gs, axis=-1, keepdims=True)
        outs.append(gs * lax.rsqrt(ms + EPS))
    gn_ref[...] = jnp.concatenate(outs, axis=1) * snw_ref[...]


def _ssd_step(proj, dt_raw, conv_state, ssm_state, hrow, conv_w, conv_b, dexp, snw, e64):
    db = proj.shape[0]
    sel = np.zeros((16, 128), np.float32)
    sel[0:3, 0] = 1.0
    sel[3:6, 1] = 1.0
    sel = jnp.asarray(sel, BF16)
    full = lambda a: pl.BlockSpec(a.shape, lambda b: (0,) * a.ndim)
    hp = SSD_HEADS // 2
    return pl.pallas_call(
        _ssd_step_kernel,
        grid=(db,),
        in_specs=[
            pl.BlockSpec((None, 1, CONV_DIM), lambda b: (b, 0, 0)),
            pl.BlockSpec((None, 1, D_INNER), lambda b: (b, 0, 2)),
            pl.BlockSpec((None, 1, 128), lambda b: (b, 0, 0)),
            pl.BlockSpec((None, SSD_CONV - 1, CONV_DIM), lambda b: (b, 0, 0)),
            pl.BlockSpec((None, hp, 128, SSD_STATE), lambda b: (b, 0, 0, 0)),
            full(hrow), full(conv_w), full(conv_b), full(dexp), full(snw), full(e64), full(sel),
        ],
        out_specs=[
            pl.BlockSpec((None, 1, D_INNER), lambda b: (b, 0, 0)),
            pl.BlockSpec((None, hp, 128, SSD_STATE), lambda b: (b, 0, 0, 0)),
            pl.BlockSpec((None, SSD_CONV - 1, CONV_DIM), lambda b: (b, 0, 0)),
        ],
        out_shape=[
            jax.ShapeDtypeStruct((db, 1, D_INNER), F32),
            jax.ShapeDtypeStruct((db, hp, 128, SSD_STATE), F32),
            jax.ShapeDtypeStruct((db, SSD_CONV - 1, CONV_DIM), F32),
        ],
        compiler_params=_params(("arbitrary",)),
        name="ssd_step",
    )(proj, proj, dt_raw, conv_state, ssm_state, hrow, conv_w, conv_b, dexp, snw, e64, sel)


def _sample_ssd_layer(x, mod, norm_w, w_in, conv_w, conv_b, dt_bias, a_log, d_skip, ssd_norm_w, w_out,
                      conv_state, ssm_state):
    db = x.shape[0]
    shift, scale, gate = _split_mod(mod)
    row = lambda v: v.reshape(1, -1)
    w_main = jnp.concatenate([w_in[:, D_INNER:D_INNER + CONV_DIM], w_in[:, :D_INNER]], axis=1).astype(BF16)
    w_dt = jnp.pad(w_in[:, D_INNER + CONV_DIM:], ((0, 0), (0, 128 - SSD_HEADS))).astype(BF16)
    proj = _norm_linear(x, row(norm_w), scale, shift, w_main, tm=db, tn=2048)
    dt_raw = _norm_linear(x, row(norm_w), scale, shift, w_dt, tm=db, tn=128)
    hrow = jnp.concatenate([row(dt_bias), row(a_log), jnp.zeros((6, SSD_HEADS), F32)], axis=0)
    dexp = jnp.repeat(d_skip, SSD_HEAD_DIM).reshape(1, D_INNER)
    gn, st, cs = _ssd_step(proj.reshape(db, 1, -1), dt_raw.reshape(db, 1, 128), conv_state,
                           ssm_state.reshape(db, SSD_HEADS // 2, 128, SSD_STATE), hrow, conv_w, row(conv_b),
                           dexp, row(ssd_norm_w), _expansion(SSD_HEADS, SSD_HEAD_DIM))
    x1 = _gated_out(gn.reshape(db, D_INNER), None, x, gate, w_out.astype(BF16), None, tm=db)
    return x1, st, cs


PAGES_PER_STEP = 8
PAGES_PER_BLOCK = MOBA_BLOCK // PAGE


def _paged_mean_kernel(pt_ref, *refs):
    pages, o_ref = refs[:PAGES_PER_STEP], refs[PAGES_PER_STEP]
    for r in range(PAGES_PER_STEP // PAGES_PER_BLOCK):
        s = jnp.sum(pages[2 * r][...], axis=0) + jnp.sum(pages[2 * r + 1][...], axis=0)
        o_ref[r] = s * (1.0 / MOBA_BLOCK)


def _paged_mean(cache, page_table):
    db, n_pages = page_table.shape
    nsteps = n_pages // PAGES_PER_STEP
    bps = PAGES_PER_STEP // PAGES_PER_BLOCK

    def page_spec(r):
        return pl.BlockSpec((None, PAGE, ATT_HEADS, ATT_HEAD_DIM),
                            lambda b, s, pt: (pt[b, s * PAGES_PER_STEP + r], 0, 0, 0))

    return pl.pallas_call(
        _paged_mean_kernel,
        grid_spec=pltpu.PrefetchScalarGridSpec(
            num_scalar_prefetch=1,
            grid=(db, nsteps),
            in_specs=[page_spec(r) for r in range(PAGES_PER_STEP)],
            out_specs=pl.BlockSpec((None, bps, ATT_HEADS, ATT_HEAD_DIM), lambda b, s, pt: (b, s, 0, 0)),
        ),
        out_shape=jax.ShapeDtypeStruct((db, n_pages // PAGES_PER_BLOCK, ATT_HEADS, ATT_HEAD_DIM), F32),
        compiler_params=_params(("arbitrary", "arbitrary")),
        name="paged_mean",
    )(page_table, *([cache] * PAGES_PER_STEP))


def _sample_select_kernel(q_ref, km_ref, o_ref):
    nb = km_ref.shape[0]
    gate = jnp.sum(km_ref[...] * q_ref[...][None, :, :], axis=-1)
    blk = lax.broadcasted_iota(jnp.int32, gate.shape, 0)
    blk_f = blk.astype(F32)
    ninf = jnp.float32(-jnp.inf)
    g = gate
    rows = []
    for _ in range(MOBA_TOPK):
        m = jnp.max(g, axis=0, keepdims=True)
        first = jnp.min(jnp.where(g == m, blk_f, float(nb)), axis=0, keepdims=True)
        rows.append(first)
        g = jnp.where(blk_f == first, ninf, g)
    rows.append(jnp.zeros((8 - MOBA_TOPK, gate.shape[1]), F32))
    o_ref[...] = jnp.concatenate(rows, axis=0).astype(jnp.int32)


def _sample_select(q, kmean):
    db, nb = kmean.shape[:2]
    return pl.pallas_call(
        _sample_select_kernel,
        grid=(db,),
        in_specs=[pl.BlockSpec((None, ATT_HEADS, ATT_HEAD_DIM), lambda b: (b, 0, 0)),
                  pl.BlockSpec((None, nb, ATT_HEADS, ATT_HEAD_DIM), lambda b: (b, 0, 0, 0))],
        out_specs=pl.BlockSpec((None, 8, ATT_HEADS), lambda b: (b, 0, 0)),
        out_shape=jax.ShapeDtypeStruct((db, 8, ATT_HEADS), jnp.int32),
        compiler_params=_params(("arbitrary",)),
        name="sample_select",
    )(q, kmean)


def _sel_copies(pt_ref, idx_ref, cache_hbm, buf, sem, b):
    out = []
    for h in range(ATT_HEADS):
        for r in range(MOBA_TOPK):
            blk = idx_ref[b, h * MOBA_TOPK + r]
            for half in range(PAGES_PER_BLOCK):
                page = pt_ref[b, blk * PAGES_PER_BLOCK + half]
                slot = (r * PAGES_PER_BLOCK + half) * PAGE
                out.append(pltpu.make_async_copy(cache_hbm.at[page, :, h, :],
                                                 buf.at[h, pl.ds(slot, PAGE), :], sem))
    return out


def _sample_attn_kernel(pt_ref, idx_ref, slopes_ref, q_ref, kn_ref, vn_ref, ck_hbm, cv_hbm, o_ref,
                        kbuf, vbuf, sems, *, past_len):
    b = pl.program_id(0)
    kcopies = _sel_copies(pt_ref, idx_ref, ck_hbm, kbuf, sems.at[0], b)
    vcopies = _sel_copies(pt_ref, idx_ref, cv_hbm, vbuf, sems.at[1], b)
    for cp in kcopies + vcopies:
        cp.start()
    for cp in kcopies + vcopies:
        cp.wait()
    nsel = MOBA_TOPK * MOBA_BLOCK
    local = lax.broadcasted_iota(jnp.int32, (1, nsel), 1)
    within = jnp.bitwise_and(local, MOBA_BLOCK - 1).astype(F32)
    which = jnp.right_shift(local, MOBA_BLOCK.bit_length() - 1)
    scale = ATT_HEAD_DIM ** -0.5
    for h in range(ATT_HEADS):
        qrow = q_ref[h:h + 1, :]
        q8 = jnp.broadcast_to(qrow, (8, ATT_HEAD_DIM)).astype(BF16)
        s = lax.dot_general(q8, kbuf[h].astype(BF16), NT, preferred_element_type=F32)[0:1, :] * scale
        start = jnp.zeros((1, nsel), F32)
        for r in range(MOBA_TOPK):
            blk_start = (idx_ref[b, h * MOBA_TOPK + r] * MOBA_BLOCK).astype(F32)
            start = jnp.where(which == r, blk_start, start)
        dist = float(past_len) - (start + within)
        s = s - slopes_ref[h] * dist
        s_self = jnp.sum(qrow.astype(BF16).astype(F32) * kn_ref[h:h + 1, :].astype(BF16).astype(F32),
                         axis=-1, keepdims=True) * scale
        m = jnp.maximum(jnp.max(s, axis=-1, keepdims=True), s_self)
        pe = jnp.exp(s - m)
        p_self = jnp.exp(s_self - m)
        l = jnp.sum(pe, axis=-1, keepdims=True) + p_self
        pv = jnp.dot(jnp.broadcast_to(pe, (8, nsel)).astype(BF16), vbuf[h].astype(BF16),
                     preferred_element_type=F32)[0:1, :]
        o_ref[h:h + 1, :] = (pv + p_self * vn_ref[h:h + 1, :]) / l


def _sample_attn(q, k_new, v_new, cache_k, cache_v, page_table, idx, slopes):
    db, n_pages = page_table.shape
    nsel = MOBA_TOPK * MOBA_BLOCK
    vec = pl.BlockSpec((None, ATT_HEADS, ATT_HEAD_DIM), lambda b, pt, ix: (b, 0, 0))
    return pl.pallas_call(
        functools.partial(_sample_attn_kernel, past_len=n_pages * PAGE),
        grid_spec=pltpu.PrefetchScalarGridSpec(
            num_scalar_prefetch=2,
            grid=(db,),
            in_specs=[pl.BlockSpec(memory_space=pltpu.SMEM), vec, vec, vec,
                      pl.BlockSpec(memory_space=pl.ANY), pl.BlockSpec(memory_space=pl.ANY)],
            out_specs=vec,
            scratch_shapes=[pltpu.VMEM((ATT_HEADS, nsel, ATT_HEAD_DIM), F32),
                            pltpu.VMEM((ATT_HEADS, nsel, ATT_HEAD_DIM), F32),
                            pltpu.SemaphoreType.DMA((2,))],
        ),
        out_shape=jax.ShapeDtypeStruct((db, ATT_HEADS, ATT_HEAD_DIM), F32),
        compiler_params=_params(("arbitrary",)),
        name="sample_attn",
    )(page_table, idx, slopes, q, k_new, v_new, cache_k, cache_v)


def _sample_moba_layer(x, mod, norm_w, w_in, w_out, final_w, cache_k, cache_v, page_table):
    db = x.shape[0]
    shift, scale, gate = _split_mod(mod)
    row = lambda v: v.reshape(1, -1)
    proj = _norm_linear(x, row(norm_w), scale, shift, w_in.astype(BF16), tm=db, tn=1024)
    hs = (db, ATT_HEADS, ATT_HEAD_DIM)
    q, k, v = (proj[:, i * ATT_WIDTH:(i + 1) * ATT_WIDTH].reshape(hs) for i in range(3))
    g = proj[:, 3 * ATT_WIDTH:]
    idx = _sample_select(q, _paged_mean(cache_k, page_table))
    idx = jnp.transpose(idx[:, :MOBA_TOPK, :], (0, 2, 1)).reshape(db, ATT_HEADS * MOBA_TOPK)
    o = _sample_attn(q, k, v, cache_k, cache_v, page_table, idx, _alibi_slopes())
    y = _gated_out(o.reshape(db, ATT_WIDTH), g, x, gate, w_out.astype(BF16), row(final_w), tm=db)
    return y, k, v


def kernel(x_prompt, x_sample, cache_k, cache_v, state_ssm, state_conv, page_table, c_prompt, c_sample,
           ada_w, ada_b, norm_w, ssd_w_in, ssd_conv_w, ssd_conv_b, ssd_dt_bias, ssd_a_log, ssd_d,
           ssd_norm_w, ssd_w_out, att_w_in, att_w_out, final_norm_w):
    bsz, t, d = x_prompt.shape
    db = x_sample.shape[0]
    c_rows = jnp.concatenate([c_prompt, c_sample, jnp.zeros((7, d), F32)], axis=0)
    mod = _ada_mod(c_rows, ada_w, ada_b)

    xp = x_prompt[0]
    xp, ssm_p, conv_p = _prompt_ssd_layer(xp, mod[0, 0], norm_w[0], ssd_w_in[0], ssd_conv_w[0], ssd_conv_b[0],
                                          ssd_dt_bias[0], ssd_a_log[0], ssd_d[0], ssd_norm_w[0], ssd_w_out[0])
    y_p, k_p, v_p = _prompt_moba_layer(xp, mod[1, 0], norm_w[1], att_w_in[0], att_w_out[0], final_norm_w)

    xs = x_sample[:, 0]
    xs, ssm_s, conv_s = _sample_ssd_layer(xs, mod[0, 1:1 + db], norm_w[0], ssd_w_in[0], ssd_conv_w[0], ssd_conv_b[0],
                                          ssd_dt_bias[0], ssd_a_log[0], ssd_d[0], ssd_norm_w[0], ssd_w_out[0],
                                          state_conv[0], state_ssm[0])
    y_s, k_s, v_s = _sample_moba_layer(xs, mod[1, 1:1 + db], norm_w[1], att_w_in[0], att_w_out[0], final_norm_w,
                                       cache_k[0], cache_v[0], page_table)

    n_pg = t // PAGE
    kv_shape = (1, bsz, n_pg, PAGE, ATT_HEADS, ATT_HEAD_DIM)
    kv_s_shape = (1, db, 1, ATT_HEADS, ATT_HEAD_DIM)
    return (y_p.reshape(bsz, t, d),
            y_s.reshape(db, 1, d),
            k_p.reshape(kv_shape), v_p.reshape(kv_shape),
            k_s.reshape(kv_s_shape), v_s.reshape(kv_s_shape),
            ssm_p.reshape(1, bsz, SSD_HEADS, SSD_HEAD_DIM, SSD_STATE),
            conv_p.reshape(1, bsz, SSD_CONV - 1, CONV_DIM),
            ssm_s.reshape(1, db, SSD_HEADS, SSD_HEAD_DIM, SSD_STATE),
            conv_s.reshape(1, db, SSD_CONV - 1, CONV_DIM))
```

```python
import functools

import numpy as np
import jax
import jax.numpy as jnp
from jax import lax
from jax.experimental import pallas as pl
from jax.experimental.pallas import tpu as pltpu

F32 = jnp.float32
BF16 = jnp.bfloat16
HI = lax.Precision.HIGHEST

D_MODEL = 1024
D_INNER = 2048
SSD_HEADS = 32
SSD_HEAD_DIM = 64
SSD_GROUPS = 8
SSD_STATE = 128
SSD_CONV = 4
CHUNK = 128
GN = SSD_GROUPS * SSD_STATE
CONV_DIM = D_INNER + 2 * GN
ATT_HEADS = 16
ATT_HEAD_DIM = 64
ATT_WIDTH = ATT_HEADS * ATT_HEAD_DIM
MOBA_BLOCK = 256
MOBA_TOPK = 3
PAGE = 128
EPS = 1e-6
NEG = -1e30

VMEM_LIMIT = 56 * 1024 * 1024

NT = (((1,), (1,)), ((), ()))
TN = (((0,), (0,)), ((), ()))


def _silu(x):
    return x / (1.0 + jnp.exp(-x))


def _softplus(x):
    return jnp.maximum(x, 0.0) + jnp.log1p(jnp.exp(-jnp.abs(x)))


def _split3(x):
    hi = x.astype(BF16)
    r1 = x - hi.astype(F32)
    mid = r1.astype(BF16)
    lo = (r1 - mid.astype(F32)).astype(BF16)
    return hi, mid, lo


def _expand_lanes(x, e_ref):
    hi, mid, lo = _split3(x)
    e = e_ref[...]
    out = jnp.dot(hi, e, preferred_element_type=F32)
    out = out + jnp.dot(mid, e, preferred_element_type=F32)
    return out + jnp.dot(lo, e, preferred_element_type=F32)


def _params(sem):
    return pltpu.CompilerParams(dimension_semantics=sem, vmem_limit_bytes=VMEM_LIMIT)


def _ada_kernel(c_ref, w_ref, b_ref, o_ref):
    sc = _silu(c_ref[...])
    o_ref[...] = jnp.dot(sc, w_ref[...], precision=HI, preferred_element_type=F32) + b_ref[...]


def _ada_mod(c_rows, ada_w, ada_b):
    depth, d, n3 = ada_w.shape
    m = c_rows.shape[0]
    tn = 1024
    return pl.pallas_call(
        _ada_kernel,
        grid=(depth, n3 // tn),
        in_specs=[
            pl.BlockSpec((m, d), lambda i, j: (0, 0)),
            pl.BlockSpec((None, d, tn), lambda i, j: (i, 0, j)),
            pl.BlockSpec((None, 1, tn), lambda i, j: (i, 0, j)),
        ],
        out_specs=pl.BlockSpec((None, m, tn), lambda i, j: (i, 0, j)),
        out_shape=jax.ShapeDtypeStruct((depth, m, n3), F32),
        compiler_params=_params(("arbitrary", "arbitrary")),
        name="ada_mod",
    )(c_rows, ada_w, ada_b.reshape(depth, 1, n3))


def _norm_mod(x, nw, scale, shift):
    ms = jnp.mean(x * x, axis=-1, keepdims=True)
    h = x * lax.rsqrt(ms + EPS) * nw
    return h * (1.0 + scale) + shift


def _norm_linear_kernel(x_ref, nw_ref, sc_ref, sh_ref, w_ref, o_ref):
    h = _norm_mod(x_ref[...], nw_ref[...], sc_ref[...], sh_ref[...])
    o_ref[...] = jnp.dot(h.astype(BF16), w_ref[...], preferred_element_type=F32)


def _norm_linear(x, nw, scale, shift, w_bf16, *, tm, tn):
    m, k = x.shape
    n = w_bf16.shape[1]
    per_row = scale.shape[0] != 1
    vec = (pl.BlockSpec((tm, k), lambda j, i: (i, 0)) if per_row
           else pl.BlockSpec((1, k), lambda j, i: (0, 0)))
    return pl.pallas_call(
        _norm_linear_kernel,
        grid=(n // tn, m // tm),
        in_specs=[
            pl.BlockSpec((tm, k), lambda j, i: (i, 0)),
            pl.BlockSpec((1, k), lambda j, i: (0, 0)),
            vec, vec,
            pl.BlockSpec((k, tn), lambda j, i: (0, j)),
        ],
        out_specs=pl.BlockSpec((tm, tn), lambda j, i: (i, j)),
        out_shape=jax.ShapeDtypeStruct((m, n), F32),
        compiler_params=_params(("arbitrary", "arbitrary")),
        name="norm_linear",
    )(x, nw, scale, shift, w_bf16)


VROWS = 80


def _k_proj_kernel(x_ref, nw_ref, sc_ref, sh_ref, w_ref, kt_ref, k16_ref, km_ref):
    h = _norm_mod(x_ref[...], nw_ref[...], sc_ref[...], sh_ref[...])
    acc = jnp.dot(h.astype(BF16), w_ref[...], preferred_element_type=F32)
    for pg in range(MOBA_BLOCK // PAGE):
        kt_ref[pg] = acc[pg * PAGE:(pg + 1) * PAGE, :].T
    k16_ref[...] = acc.astype(BF16)
    km_ref[...] = jnp.mean(acc, axis=0, keepdims=True)


def _v_proj_kernel(x_ref, nw_ref, sc_ref, sh_ref, w_ref, vt_ref, vh_ref):
    h = _norm_mod(x_ref[...], nw_ref[...], sc_ref[...], sh_ref[...])
    acc = jnp.dot(h.astype(BF16), w_ref[...], preferred_element_type=F32)
    acc_t = acc.T
    for pg in range(MOBA_BLOCK // PAGE):
        vt_ref[pg] = acc_t[:, pg * PAGE:(pg + 1) * PAGE]
    pad = jnp.concatenate([jnp.ones((1, MOBA_BLOCK), F32),
                           jnp.zeros((VROWS - ATT_HEAD_DIM - 1, MOBA_BLOCK), F32)], axis=0)
    for hd in range(ATT_HEADS):
        rows = acc_t[hd * ATT_HEAD_DIM:(hd + 1) * ATT_HEAD_DIM, :]
        vh_ref[hd] = jnp.concatenate([rows, pad], axis=0).astype(BF16)


def _kv_proj(x, nw, scale, shift, w_bf16, which):
    t, k = x.shape
    w = w_bf16.shape[1]
    B = MOBA_BLOCK
    nb = t // B
    ppb = B // PAGE
    one = pl.BlockSpec((1, k), lambda i: (0, 0))
    paged = (jax.ShapeDtypeStruct((t // PAGE, w, PAGE), F32), pl.BlockSpec((ppb, w, PAGE), lambda i: (i, 0, 0)))
    if which == "k":
        body = _k_proj_kernel
        outs = [paged,
                (jax.ShapeDtypeStruct((t, w), BF16), pl.BlockSpec((B, w), lambda i: (i, 0))),
                (jax.ShapeDtypeStruct((nb, 1, w), F32), pl.BlockSpec((None, 1, w), lambda i: (i, 0, 0)))]
    else:
        body = _v_proj_kernel
        outs = [paged,
                (jax.ShapeDtypeStruct((nb, ATT_HEADS, VROWS, B), BF16),
                 pl.BlockSpec((None, ATT_HEADS, VROWS, B), lambda i: (i, 0, 0, 0)))]
    return pl.pallas_call(
        body,
        grid=(nb,),
        in_specs=[pl.BlockSpec((B, k), lambda i: (i, 0)), one, one, one,
                  pl.BlockSpec((k, w), lambda i: (0, 0))],
        out_specs=[o[1] for o in outs],
        out_shape=[o[0] for o in outs],
        compiler_params=_params(("arbitrary",)),
        name=which + "_proj",
    )(x, nw, scale, shift, w_bf16)


def _gated_out_kernel(*refs, has_g, has_final):
    it = iter(refs)
    a_ref = next(it)
    g_ref = next(it) if has_g else None
    x_ref, gate_ref, w_ref = next(it), next(it), next(it)
    fw_ref = next(it) if has_final else None
    o_ref = next(it)
    a = a_ref[...].astype(F32)
    if has_g:
        a = a * _silu(g_ref[...])
    y = x_ref[...] + gate_ref[...] * jnp.dot(a.astype(BF16), w_ref[...], preferred_element_type=F32)
    if has_final:
        ms = jnp.mean(y * y, axis=-1, keepdims=True)
        y = y * lax.rsqrt(ms + EPS) * fw_ref[...]
    o_ref[...] = y


def _gated_out(a, g, x, gate, w_bf16, final_w, *, tm):
    m, k = a.shape
    n = w_bf16.shape[1]
    per_row = gate.shape[0] != 1
    row = lambda width: pl.BlockSpec((tm, width), lambda i: (i, 0))
    one = lambda width: pl.BlockSpec((1, width), lambda i: (0, 0))
    ins, specs = [a], [row(k)]
    if g is not None:
        ins.append(g)
        specs.append(row(k))
    ins += [x, gate, w_bf16]
    specs += [row(n), row(n) if per_row else one(n), pl.BlockSpec((k, n), lambda i: (0, 0))]
    if final_w is not None:
        ins.append(final_w)
        specs.append(one(n))
    return pl.pallas_call(
        functools.partial(_gated_out_kernel, has_g=g is not None, has_final=final_w is not None),
        grid=(m // tm,),
        in_specs=specs,
        out_specs=row(n),
        out_shape=jax.ShapeDtypeStruct((m, n), F32),
        compiler_params=_params(("arbitrary",)),
        name="gated_out",
    )(*ins)


def _ssd_prompt_kernel(x_ref, xbc_ref, z_ref, modv_ref, wdt_ref, wdtT_ref, hrow_ref, hcol_ref,
                       cw_ref, cb_ref, dexp_ref, snw_ref, e64_ref, e128_ref, wout_ref,
                       xo_ref, st_ref, tail_ref,
                       xpad, xg_s, b_s, c_s, xdt_s, abc_s, acst_s, y_s, gn_s):
    L = CHUNK
    ci = pl.program_id(0)
    nci = pl.num_programs(0)

    @pl.when(ci == 0)
    def _():
        xpad[0:8, :] = jnp.zeros((8, CONV_DIM), F32)
        st_ref[...] = jnp.zeros(st_ref.shape, F32)

    x = x_ref[...]
    hn = _norm_mod(x, modv_ref[0:1, :], modv_ref[1:2, :], modv_ref[2:3, :]).astype(BF16)
    dt_raw = jnp.dot(hn, wdt_ref[...], preferred_element_type=F32)
    dt_rawT = lax.dot_general(wdtT_ref[...], hn, NT, preferred_element_type=F32)
    dt = _softplus(dt_raw + hrow_ref[0:1, :])
    dtT = _softplus(dt_rawT + hcol_ref[:, 0:1])
    dA = dt * (-jnp.exp(hrow_ref[1:2, :]))
    dAT = dtT * (-jnp.exp(hcol_ref[:, 1:2]))
    r_i = lax.broadcasted_iota(jnp.int32, (L, L), 0)
    c_i = lax.broadcasted_iota(jnp.int32, (L, L), 1)
    tril = (r_i >= c_i).astype(F32)
    triu = (r_i <= c_i).astype(F32)
    acs = jnp.dot(tril, dA, precision=HI, preferred_element_type=F32)
    acsT = jnp.dot(dAT, triu, precision=HI, preferred_element_type=F32)
    for h in range(SSD_HEADS):
        acst_s[h] = acsT[h:h + 1, :]
    abig = _expand_lanes(acs, e128_ref)
    for h in range(SSD_HEADS):
        abc_s[h] = abig[:, h * 128:(h + 1) * 128]
    dt_exp = _expand_lanes(dt, e64_ref)

    xpad[8:8 + L, :] = xbc_ref[...]
    cw = 512
    for blk in range(CONV_DIM // cw):
        cs = slice(blk * cw, (blk + 1) * cw)
        acc = cb_ref[:, cs] + xpad[5:5 + L, cs] * cw_ref[0:1, cs]
        acc = acc + xpad[6:6 + L, cs] * cw_ref[1:2, cs]
        acc = acc + xpad[7:7 + L, cs] * cw_ref[2:3, cs]
        acc = acc + xpad[8:8 + L, cs] * cw_ref[3:4, cs]
        xc = _silu(acc)
        if blk < 4:
            for q in range(2):
                g = blk * 2 + q
                xg = xc[:, q * 256:(q + 1) * 256]
                xg_s[g] = xg
                xd = xg * dt_exp[:, g * 256:(g + 1) * 256]
                xdt_s[2 * g] = xd[:, 0:128]
                xdt_s[2 * g + 1] = xd[:, 128:256]
        elif blk < 6:
            for q in range(4):
                b_s[(blk - 4) * 4 + q] = xc[:, q * 128:(q + 1) * 128].astype(BF16)
        else:
            for q in range(4):
                c_s[(blk - 6) * 4 + q] = xc[:, q * 128:(q + 1) * 128]

    @pl.when(ci == nci - 1)
    def _():
        tail_ref[...] = xpad[5 + L:8 + L, :]

    xpad[5:8, :] = xpad[5 + L:8 + L, :]

    causal = r_i >= c_i
    lane = lax.broadcasted_iota(jnp.int32, (L, 128), 1)
    sub = lax.broadcasted_iota(jnp.int32, (128, 128), 0)
    lo_lane = lane < 64
    lo_sub = sub < 64

    def group_body(g, carry):
        bb = b_s[g]
        cg = c_s[g]
        cbm = lax.dot_general(cg.astype(BF16), bb, NT, preferred_element_type=F32)
        for pr in range(2):
            pidx = 2 * g + pr
            xdt_pair = xdt_s[pidx]
            hprev = st_ref[pidx]
            hprev_b = hprev.astype(BF16)
            ypair = jnp.zeros((L, 128), F32)
            dends, cds = [], []
            for hh in range(2):
                h = 2 * pidx + hh
                abc = abc_s[h]
                arow = acst_s[h]
                dec = jnp.exp(jnp.where(causal, abc - arow, NEG))
                mh = (cbm * dec).astype(BF16)
                keep = lo_lane if hh == 0 else jnp.logical_not(lo_lane)
                xm = jnp.where(keep, xdt_pair, 0.0).astype(BF16)
                ypair = ypair + jnp.dot(mh, xm, preferred_element_type=F32)
                cd = (cg * jnp.exp(abc)).astype(BF16)
                yo = lax.dot_general(cd, hprev_b, NT, preferred_element_type=F32)
                ypair = ypair + jnp.where(keep, yo, 0.0)
                tot = abc[L - 1:L, :]
                dends.append(jnp.exp(tot - abc))
                cds.append(jnp.exp(tot))
            y_s[pidx] = ypair
            dend = jnp.where(lo_lane, dends[0], dends[1])
            xw = (xdt_pair * dend).astype(BF16)
            states = lax.dot_general(xw, bb, TN, preferred_element_type=F32)
            cdm = jnp.where(lo_sub, jnp.broadcast_to(cds[0], (128, 128)),
                            jnp.broadcast_to(cds[1], (128, 128)))
            st_ref[pidx] = hprev * cdm + states
        return carry

    lax.fori_loop(0, SSD_GROUPS, group_body, 0)

    for g in range(SSD_GROUPS):
        cs = slice(g * 256, (g + 1) * 256)
        yg = jnp.concatenate([y_s[2 * g], y_s[2 * g + 1]], axis=1)
        gg = (yg + xg_s[g] * dexp_ref[:, cs]) * _silu(z_ref[:, cs])
        ms = jnp.mean(gg * gg, axis=-1, keepdims=True)
        gn_s[:, cs] = (gg * lax.rsqrt(ms + EPS) * snw_ref[:, cs]).astype(BF16)

    xo_ref[...] = x + modv_ref[3:4, :] * jnp.dot(gn_s[...], wout_ref[...], preferred_element_type=F32)


def _ssd_prompt(x, proj, modv, wdt, wdtT, hrow, hcol, conv_w, conv_b, dexp, snw, e64, e128, wout):
    t = x.shape[0]
    nc = t // CHUNK
    L = CHUNK
    full = lambda a: pl.BlockSpec(a.shape, lambda c: (0,) * a.ndim)
    return pl.pallas_call(
        _ssd_prompt_kernel,
        grid=(nc,),
        in_specs=[
            pl.BlockSpec((L, D_MODEL), lambda c: (c, 0)),
            pl.BlockSpec((L, CONV_DIM), lambda c: (c, 0)),
            pl.BlockSpec((L, D_INNER), lambda c: (c, 2)),
            full(modv), full(wdt), full(wdtT), full(hrow), full(hcol), full(conv_w), full(conv_b),
            full(dexp), full(snw), full(e64), full(e128), full(wout),
        ],
        out_specs=[
            pl.BlockSpec((L, D_MODEL), lambda c: (c, 0)),
            pl.BlockSpec((SSD_HEADS // 2, 128, SSD_STATE), lambda c: (0, 0, 0)),
            pl.BlockSpec((SSD_CONV - 1, CONV_DIM), lambda c: (0, 0)),
        ],
        out_shape=[
            jax.ShapeDtypeStruct((t, D_MODEL), F32),
            jax.ShapeDtypeStruct((SSD_HEADS // 2, 128, SSD_STATE), F32),
            jax.ShapeDtypeStruct((SSD_CONV - 1, CONV_DIM), F32),
        ],
        scratch_shapes=[
            pltpu.VMEM((8 + L, CONV_DIM), F32),
            pltpu.VMEM((SSD_GROUPS, L, 256), F32),
            pltpu.VMEM((SSD_GROUPS, L, SSD_STATE), BF16),
            pltpu.VMEM((SSD_GROUPS, L, SSD_STATE), F32),
            pltpu.VMEM((SSD_HEADS // 2, L, 128), F32),
            pltpu.VMEM((SSD_HEADS, L, 128), F32),
            pltpu.VMEM((SSD_HEADS, 1, L), F32),
            pltpu.VMEM((SSD_HEADS // 2, L, 128), F32),
            pltpu.VMEM((L, D_INNER), BF16),
        ],
        compiler_params=_params(("arbitrary",)),
        name="ssd_prompt",
    )(x, proj, proj, modv, wdt, wdtT, hrow, hcol, conv_w, conv_b, dexp, snw, e64, e128, wout)


def _expansion(heads, width):
    e = np.zeros((heads, heads * width), np.float32)
    for h in range(heads):
        e[h, h * width:(h + 1) * width] = 1.0
    return jnp.asarray(e, BF16)


def _split_mod(mod):
    return mod[..., :D_MODEL], mod[..., D_MODEL:2 * D_MODEL], mod[..., 2 * D_MODEL:]


def _prompt_ssd_layer(x, mod, norm_w, w_in, conv_w, conv_b, dt_bias, a_log, d_skip, ssd_norm_w, w_out):
    shift, scale, gate = _split_mod(mod)
    row = lambda v: v.reshape(1, -1)
    w_main = jnp.concatenate([w_in[:, D_INNER:D_INNER + CONV_DIM], w_in[:, :D_INNER]], axis=1).astype(BF16)
    w_dt = w_in[:, D_INNER + CONV_DIM:].astype(BF16)
    proj = _norm_linear(x, row(norm_w), row(scale), row(shift), w_main, tm=512, tn=2048)
    zeros = jnp.zeros((4, D_MODEL), F32)
    modv = jnp.concatenate([row(norm_w), row(scale), row(shift), row(gate), zeros], axis=0)
    hrow = jnp.concatenate([row(dt_bias), row(a_log), jnp.zeros((6, SSD_HEADS), F32)], axis=0)
    hcol = hrow.T
    dexp = jnp.repeat(d_skip, SSD_HEAD_DIM).reshape(1, D_INNER)
    return _ssd_prompt(x, proj, modv, w_dt, w_dt.T, hrow, hcol, conv_w, row(conv_b), dexp,
                       row(ssd_norm_w), _expansion(SSD_HEADS, SSD_HEAD_DIM),
                       _expansion(SSD_HEADS, 128), w_out.astype(BF16))


def _topk_block_mask(gate, blk, n_valid):
    nb = gate.shape[0]
    ninf = jnp.float32(-jnp.inf)
    blk_f = blk.astype(F32)
    g = jnp.where(blk < n_valid, gate, ninf)
    sel = jnp.zeros(gate.shape, jnp.bool_)
    for _ in range(MOBA_TOPK):
        m = jnp.max(g, axis=0, keepdims=True)
        first = jnp.min(jnp.where(g == m, blk_f, float(nb)), axis=0, keepdims=True)
        pick = jnp.logical_and(blk_f == first, m > ninf)
        sel = jnp.logical_or(sel, pick)
        g = jnp.where(pick, ninf, g)
    return jnp.where(sel, 0.0, NEG)


def _moba_select_kernel(q_ref, km_ref, o_ref):
    c = pl.program_id(0)
    nb = km_ref.shape[0]
    tq = q_ref.shape[0]
    lane = lax.broadcasted_iota(jnp.int32, (tq, 128), 1)
    blk = lax.broadcasted_iota(jnp.int32, (nb, tq), 0)
    for p in range(ATT_HEADS // 2):
        qp = q_ref[:, p * 128:(p + 1) * 128]
        kmp = km_ref[:, p * 128:(p + 1) * 128]
        for hh in range(2):
            keep = (lane < 64) if hh == 0 else (lane >= 64)
            qm = jnp.where(keep, qp, 0.0)
            gate = lax.dot_general(kmp, qm, NT, precision=HI, preferred_element_type=F32)
            o_ref[2 * p + hh] = _topk_block_mask(gate, blk, c)


def _moba_select(q, kmean):
    t, w = q.shape
    nb = kmean.shape[0]
    return pl.pallas_call(
        _moba_select_kernel,
        grid=(nb,),
        in_specs=[pl.BlockSpec((MOBA_BLOCK, w), lambda c: (c, 0)),
                  pl.BlockSpec((nb, w), lambda c: (0, 0))],
        out_specs=pl.BlockSpec((ATT_HEADS, nb, MOBA_BLOCK), lambda c: (0, 0, c)),
        out_shape=jax.ShapeDtypeStruct((ATT_HEADS, nb, t), F32),
        compiler_params=_params(("arbitrary",)),
        name="moba_select",
    )(q, kmean)


LOG2E = 1.4426950408889634


def _moba_attn_kernel(slopes_ref, q_ref, k_ref, vh_ref, mask_ref, o_ref, spa_s, spb_s):
    B = MOBA_BLOCK
    hd = ATT_HEAD_DIM
    p = pl.program_id(0)
    c = pl.program_id(1)
    q = q_ref[...] * (hd ** -0.5 * LOG2E)
    lane = lax.broadcasted_iota(jnp.int32, (B, 128), 1)
    qh = [jnp.where(lane < hd, q, 0.0).astype(BF16), jnp.where(lane >= hd, q, 0.0).astype(BF16)]
    s_i = lax.broadcasted_iota(jnp.int32, (B, B), 0)
    t_i = lax.broadcasted_iota(jnp.int32, (B, B), 1)
    dloc = (s_i - t_i).astype(F32)
    sl = [slopes_ref[2 * p] * LOG2E, slopes_ref[2 * p + 1] * LOG2E]
    bh = [dloc * sl[0], dloc * sl[1]]

    def scores(j, hh):
        return lax.dot_general(k_ref[j], qh[hh], NT, preferred_element_type=F32) + bh[hh]

    ms, accs = [], []
    for hh in range(2):
        sp = jnp.where(s_i <= t_i, scores(c, hh), NEG)
        m = jnp.max(sp, axis=0, keepdims=True)
        ms.append(m)
        accs.append(jnp.dot(vh_ref[c, hh], jnp.exp2(sp - m).astype(BF16), preferred_element_type=F32))

    combos = [(hh, n) for hh in range(2) for n in range(2)]

    def trip_blocks(t):
        return [jnp.minimum(2 * t + n, c) for n in range(2)]

    def put_scores(t, buf):
        js = trip_blocks(t)
        for k, (hh, n) in enumerate(combos):
            buf[k] = scores(js[n], hh)

    def softmax_pv(t, buf, carry):
        m_old, acc_old = carry[:2], carry[2:]
        js = trip_blocks(t)
        radd = {(hh, n): mask_ref[hh, pl.ds(js[n], 1), :] - (c - js[n]).astype(F32) * float(B) * sl[hh]
                for hh, n in combos}
        mx = {cm: jnp.max(buf[k], axis=0, keepdims=True) + radd[cm] for k, cm in enumerate(combos)}
        m_new = [jnp.maximum(m_old[hh], jnp.maximum(mx[(hh, 0)], mx[(hh, 1)])) for hh in range(2)]
        pe = {(hh, n): jnp.exp2(buf[k] - (m_new[hh] - radd[(hh, n)])).astype(BF16)
              for k, (hh, n) in enumerate(combos)}
        pv = {(hh, n): jnp.dot(vh_ref[js[n], hh], pe[(hh, n)], preferred_element_type=F32) for hh, n in combos}
        acc_new = [acc_old[hh] * jnp.exp2(m_old[hh] - m_new[hh]) + (pv[(hh, 0)] + pv[(hh, 1)])
                   for hh in range(2)]
        return tuple(m_new) + tuple(acc_new)

    put_scores(0, spa_s)

    def body(i, carry):
        put_scores(2 * i + 1, spb_s)
        carry = softmax_pv(2 * i, spa_s, carry)
        put_scores(2 * i + 2, spa_s)
        return softmax_pv(2 * i + 1, spb_s, carry)

    res = lax.fori_loop(0, lax.shift_right_logical(c + 3, 2), body, tuple(ms) + tuple(accs))
    outs = [acc[0:hd, :] / acc[hd:hd + 1, :] for acc in res[2:4]]
    o_ref[...] = jnp.concatenate(outs, axis=0).T


def _moba_attn(q, k_blocks, vh_blocks, mask, slopes):
    t, w = q.shape
    nb = k_blocks.shape[0]
    B = MOBA_BLOCK
    return pl.pallas_call(
        _moba_attn_kernel,
        grid=(ATT_HEADS // 2, nb),
        in_specs=[
            pl.BlockSpec(memory_space=pltpu.SMEM),
            pl.BlockSpec((B, 128), lambda p, c: (c, p)),
            pl.BlockSpec((nb, B, 128), lambda p, c: (0, 0, p)),
            pl.BlockSpec((nb, 2, VROWS, B), lambda p, c: (0, p, 0, 0)),
            pl.BlockSpec((2, nb, B), lambda p, c: (p, 0, c)),
        ],
        out_specs=pl.BlockSpec((B, 128), lambda p, c: (c, p)),
        out_shape=jax.ShapeDtypeStruct((t, w), F32),
        scratch_shapes=[pltpu.VMEM((4, B, B), F32), pltpu.VMEM((4, B, B), F32)],
        compiler_params=_params(("arbitrary", "arbitrary")),
        name="moba_attn",
    )(slopes, q, k_blocks, vh_blocks, mask)


def _alibi_slopes():
    return jnp.asarray(2.0 ** (-8.0 * np.arange(1, ATT_HEADS + 1) / ATT_HEADS), dtype=F32)


def _prompt_moba_layer(x, mod, norm_w, w_in, w_out, final_w):
    shift, scale, gate = _split_mod(mod)
    row = lambda v: v.reshape(1, -1)
    nw, sc, sh = row(norm_w), row(scale), row(shift)
    wq, wk, wv, wg = (w_in[:, i * ATT_WIDTH:(i + 1) * ATT_WIDTH].astype(BF16) for i in range(4))
    B = MOBA_BLOCK
    nb = x.shape[0] // B
    q = _norm_linear(x, nw, sc, sh, wq, tm=512, tn=1024)
    g = _norm_linear(x, nw, sc, sh, wg, tm=512, tn=1024)
    kt, k16, kmean = _kv_proj(x, nw, sc, sh, wk, "k")
    vt, vh16 = _kv_proj(x, nw, sc, sh, wv, "v")
    mask = _moba_select(q, kmean.reshape(nb, ATT_WIDTH))
    o = _moba_attn(q, k16.reshape(nb, B, ATT_WIDTH), vh16, mask, _alibi_slopes())
    y = _gated_out(o, g, x, row(gate), w_out.astype(BF16), row(final_w), tm=512)
    return y, kt, vt


def _ssd_step_kernel(xbc_ref, z_ref, dtr_ref, cst_ref, st_ref, hrow_ref, cw_ref, cb_ref, dexp_ref, snw_ref,
                     e64_ref, sel_ref, gn_ref, sto_ref, cso_ref):
    xbc = xbc_ref[...]
    conv = cb_ref[...] + cst_ref[0:1, :] * cw_ref[0:1, :]
    conv = conv + cst_ref[1:2, :] * cw_ref[1:2, :]
    conv = conv + cst_ref[2:3, :] * cw_ref[2:3, :]
    conv = conv + xbc * cw_ref[3:4, :]
    cso_ref[0:2, :] = cst_ref[1:3, :]
    cso_ref[2:3, :] = xbc
    xc = _silu(conv)
    xs = xc[:, :D_INNER]
    dt = _softplus(dtr_ref[:, :SSD_HEADS] + hrow_ref[0:1, :])
    dec = jnp.exp(dt * (-jnp.exp(hrow_ref[1:2, :])))
    both = _expand_lanes(jnp.concatenate([dt, dec, jnp.zeros((6, SSD_HEADS), F32)], axis=0), e64_ref)
    xdt = xs * both[0:1, :]
    dec_exp = both[1:2, :]
    pieces = [p.astype(F32) for p in _split3(xdt) + _split3(dec_exp)]
    stack = jnp.concatenate(pieces + [jnp.zeros((10, D_INNER), F32)], axis=0).astype(BF16)
    cols = lax.dot_general(stack, sel_ref[...], TN, preferred_element_type=F32)
    ys = []
    for i in range(SSD_HEADS // 2):
        g = i // 2
        brow = xc[:, D_INNER + g * 128:D_INNER + (g + 1) * 128]
        crow = xc[:, D_INNER + GN + g * 128:D_INNER + GN + (g + 1) * 128]
        xcol = cols[i * 128:(i + 1) * 128, 0:1]
        dcol = cols[i * 128:(i + 1) * 128, 1:2]
        hn = st_ref[i] * dcol + xcol * brow
        sto_ref[i] = hn
        c8 = jnp.broadcast_to(crow, (8, SSD_STATE)).astype(BF16)
        ys.append(lax.dot_general(c8, hn.astype(BF16), NT, preferred_element_type=F32)[0:1, :])
    y = jnp.concatenate(ys, axis=1)
    gg = (y + xs * dexp_ref[...]) * _silu(z_ref[...])
    outs = []
    for g in range(SSD_GROUPS):
        gs = gg[:, g * 256:(g + 1) * 256]
        ms = jnp.mean(gs * gs, axis=-1, keepdims=True)
        outs.append(gs * lax.rsqrt(ms + EPS))
    gn_ref[...] = jnp.concatenate(outs, axis=1) * snw_ref[...]


def _ssd_step(proj, dt_raw, conv_state, ssm_state, hrow, conv_w, conv_b, dexp, snw, e64):
    db = proj.shape[0]
    sel = np.zeros((16, 128), np.float32)
    sel[0:3, 0] = 1.0
    sel[3:6, 1] = 1.0
    sel = jnp.asarray(sel, BF16)
    full = lambda a: pl.BlockSpec(a.shape, lambda b: (0,) * a.ndim)
    hp = SSD_HEADS // 2
    return pl.pallas_call(
        _ssd_step_kernel,
        grid=(db,),
        in_specs=[
            pl.BlockSpec((None, 1, CONV_DIM), lambda b: (b, 0, 0)),
            pl.BlockSpec((None, 1, D_INNER), lambda b: (b, 0, 2)),
            pl.BlockSpec((None, 1, 128), lambda b: (b, 0, 0)),
            pl.BlockSpec((None, SSD_CONV - 1, CONV_DIM), lambda b: (b, 0, 0)),
            pl.BlockSpec((None, hp, 128, SSD_STATE), lambda b: (b, 0, 0, 0)),
            full(hrow), full(conv_w), full(conv_b), full(dexp), full(snw), full(e64), full(sel),
        ],
        out_specs=[
            pl.BlockSpec((None, 1, D_INNER), lambda b: (b, 0, 0)),
            pl.BlockSpec((None, hp, 128, SSD_STATE), lambda b: (b, 0, 0, 0)),
            pl.BlockSpec((None, SSD_CONV - 1, CONV_DIM), lambda b: (b, 0, 0)),
        ],
        out_shape=[
            jax.ShapeDtypeStruct((db, 1, D_INNER), F32),
            jax.ShapeDtypeStruct((db, hp, 128, SSD_STATE), F32),
            jax.ShapeDtypeStruct((db, SSD_CONV - 1, CONV_DIM), F32),
        ],
        compiler_params=_params(("arbitrary",)),
        name="ssd_step",
    )(proj, proj, dt_raw, conv_state, ssm_state, hrow, conv_w, conv_b, dexp, snw, e64, sel)


def _sample_ssd_layer(x, mod, norm_w, w_in, conv_w, conv_b, dt_bias, a_log, d_skip, ssd_norm_w, w_out,
                      conv_state, ssm_state):
    db = x.shape[0]
    shift, scale, gate = _split_mod(mod)
    row = lambda v: v.reshape(1, -1)
    w_main = jnp.concatenate([w_in[:, D_INNER:D_INNER + CONV_DIM], w_in[:, :D_INNER]], axis=1).astype(BF16)
    w_dt = jnp.pad(w_in[:, D_INNER + CONV_DIM:], ((0, 0), (0, 128 - SSD_HEADS))).astype(BF16)
    proj = _norm_linear(x, row(norm_w), scale, shift, w_main, tm=db, tn=2048)
    dt_raw = _norm_linear(x, row(norm_w), scale, shift, w_dt, tm=db, tn=128)
    hrow = jnp.concatenate([row(dt_bias), row(a_log), jnp.zeros((6, SSD_HEADS), F32)], axis=0)
    dexp = jnp.repeat(d_skip, SSD_HEAD_DIM).reshape(1, D_INNER)
    gn, st, cs = _ssd_step(proj.reshape(db, 1, -1), dt_raw.reshape(db, 1, 128), conv_state,
                           ssm_state.reshape(db, SSD_HEADS // 2, 128, SSD_STATE), hrow, conv_w, row(conv_b),
                           dexp, row(ssd_norm_w), _expansion(SSD_HEADS, SSD_HEAD_DIM))
    x1 = _gated_out(gn.reshape(db, D_INNER), None, x, gate, w_out.astype(BF16), None, tm=db)
    return x1, st, cs


PAGES_PER_STEP = 8
PAGES_PER_BLOCK = MOBA_BLOCK // PAGE


def _paged_mean_kernel(pt_ref, *refs):
    pages, o_ref = refs[:PAGES_PER_STEP], refs[PAGES_PER_STEP]
    s = pl.program_id(1)
    w, nb = o_ref.shape
    bps = PAGES_PER_STEP // PAGES_PER_BLOCK

    @pl.when(s == 0)
    def _():
        o_ref[...] = jnp.zeros((w, nb), F32)

    lane = lax.broadcasted_iota(jnp.int32, (w, nb), 1)
    cur = o_ref[...]
    for r in range(bps):
        tot = pages[2 * r][...].reshape(w, PAGE) + pages[2 * r + 1][...].reshape(w, PAGE)
        col = jnp.sum(tot, axis=-1, keepdims=True) * (1.0 / MOBA_BLOCK)
        cur = jnp.where(lane == s * bps + r, col, cur)
    o_ref[...] = cur


def _paged_mean(cache_t, page_table):
    db, n_pages = page_table.shape
    nsteps = n_pages // PAGES_PER_STEP
    nblk = n_pages // PAGES_PER_BLOCK

    def page_spec(r):
        return pl.BlockSpec((None, ATT_HEADS, ATT_HEAD_DIM, PAGE),
                            lambda b, s, pt: (pt[b, s * PAGES_PER_STEP + r], 0, 0, 0))

    return pl.pallas_call(
        _paged_mean_kernel,
        grid_spec=pltpu.PrefetchScalarGridSpec(
            num_scalar_prefetch=1,
            grid=(db, nsteps),
            in_specs=[page_spec(r) for r in range(PAGES_PER_STEP)],
            out_specs=pl.BlockSpec((None, ATT_WIDTH, nblk), lambda b, s, pt: (b, 0, 0)),
        ),
        out_shape=jax.ShapeDtypeStruct((db, ATT_WIDTH, nblk), F32),
        compiler_params=_params(("arbitrary", "arbitrary")),
        name="paged_mean",
    )(page_table, *([cache_t] * PAGES_PER_STEP))


def _sample_select_kernel(q_ref, km_ref, o_ref):
    w, nb = km_ref.shape
    prod = km_ref[...] * q_ref[...]
    gate = jnp.sum(prod.reshape(ATT_HEADS, ATT_HEAD_DIM, nb), axis=1)
    blk_f = lax.broadcasted_iota(jnp.int32, gate.shape, 1).astype(F32)
    lane = lax.broadcasted_iota(jnp.int32, (ATT_HEADS, 128), 1)
    ninf = jnp.float32(-jnp.inf)
    g = gate
    out = jnp.zeros((ATT_HEADS, 128), F32)
    for r in range(MOBA_TOPK):
        m = jnp.max(g, axis=-1, keepdims=True)
        first = jnp.min(jnp.where(g == m, blk_f, float(nb)), axis=-1, keepdims=True)
        out = jnp.where(lane == r, first, out)
        g = jnp.where(blk_f == first, ninf, g)
    o_ref[...] = out.astype(jnp.int32)


def _sample_select(q_col, kmean_t):
    db, w, nb = kmean_t.shape
    return pl.pallas_call(
        _sample_select_kernel,
        grid=(db,),
        in_specs=[pl.BlockSpec((None, w, 1), lambda b: (b, 0, 0)),
                  pl.BlockSpec((None, w, nb), lambda b: (b, 0, 0))],
        out_specs=pl.BlockSpec((None, ATT_HEADS, 128), lambda b: (b, 0, 0)),
        out_shape=jax.ShapeDtypeStruct((db, ATT_HEADS, 128), jnp.int32),
        compiler_params=_params(("arbitrary",)),
        name="sample_select",
    )(q_col, kmean_t)


def _sel_copies(pt_ref, idx_ref, cache_hbm, buf, sem, b):
    out = []
    for h in range(ATT_HEADS):
        for r in range(MOBA_TOPK):
            blk = idx_ref[b, h * MOBA_TOPK + r]
            for half in range(PAGES_PER_BLOCK):
                page = pt_ref[b, blk * PAGES_PER_BLOCK + half]
                out.append(pltpu.make_async_copy(cache_hbm.at[page, h],
                                                 buf.at[h, r * PAGES_PER_BLOCK + half], sem))
    return out


def _sample_attn_kernel(pt_ref, idx_ref, slopes_ref, q_ref, kn_ref, vn_ref, ck_hbm, cv_hbm, o_ref,
                        kbuf, vbuf, sems, *, past_len):
    b = pl.program_id(0)
    kcopies = _sel_copies(pt_ref, idx_ref, ck_hbm, kbuf, sems.at[0], b)
    vcopies = _sel_copies(pt_ref, idx_ref, cv_hbm, vbuf, sems.at[1], b)
    for cp in kcopies + vcopies:
        cp.start()
    for cp in kcopies + vcopies:
        cp.wait()
    nsel = MOBA_TOPK * MOBA_BLOCK
    npg = MOBA_TOPK * PAGES_PER_BLOCK
    local = lax.broadcasted_iota(jnp.int32, (1, nsel), 1)
    within = jnp.bitwise_and(local, MOBA_BLOCK - 1).astype(F32)
    which = jnp.right_shift(local, MOBA_BLOCK.bit_length() - 1)
    scale = ATT_HEAD_DIM ** -0.5
    for h in range(ATT_HEADS):
        qrow = q_ref[h:h + 1, :]
        q8 = jnp.broadcast_to(qrow, (8, ATT_HEAD_DIM)).astype(BF16)
        kcat = jnp.concatenate([kbuf[h, i] for i in range(npg)], axis=1).astype(BF16)
        vcat = jnp.concatenate([vbuf[h, i] for i in range(npg)], axis=1).astype(BF16)
        s = jnp.dot(q8, kcat, preferred_element_type=F32)[0:1, :] * scale
        start = jnp.zeros((1, nsel), F32)
        for r in range(MOBA_TOPK):
            blk_start = (idx_ref[b, h * MOBA_TOPK + r] * MOBA_BLOCK).astype(F32)
            start = jnp.where(which == r, blk_start, start)
        dist = float(past_len) - (start + within)
        s = s - slopes_ref[h] * dist
        s_self = jnp.sum(qrow.astype(BF16).astype(F32) * kn_ref[h:h + 1, :].astype(BF16).astype(F32),
                         axis=-1, keepdims=True) * scale
        m = jnp.maximum(jnp.max(s, axis=-1, keepdims=True), s_self)
        pe = jnp.exp(s - m)
        p_self = jnp.exp(s_self - m)
        l = jnp.sum(pe, axis=-1, keepdims=True) + p_self
        pv = lax.dot_general(jnp.broadcast_to(pe, (8, nsel)).astype(BF16), vcat, NT,
                             preferred_element_type=F32)[0:1, :]
        o_ref[h:h + 1, :] = (pv + p_self * vn_ref[h:h + 1, :]) / l


def _sample_attn(q, k_new, v_new, cache_kt, cache_vt, page_table, idx, slopes):
    db, n_pages = page_table.shape
    npg = MOBA_TOPK * PAGES_PER_BLOCK
    vec = pl.BlockSpec((None, ATT_HEADS, ATT_HEAD_DIM), lambda b, pt, ix: (b, 0, 0))
    return pl.pallas_call(
        functools.partial(_sample_attn_kernel, past_len=n_pages * PAGE),
        grid_spec=pltpu.PrefetchScalarGridSpec(
            num_scalar_prefetch=2,
            grid=(db,),
            in_specs=[pl.BlockSpec(memory_space=pltpu.SMEM), vec, vec, vec,
                      pl.BlockSpec(memory_space=pl.ANY), pl.BlockSpec(memory_space=pl.ANY)],
            out_specs=vec,
            scratch_shapes=[pltpu.VMEM((ATT_HEADS, npg, ATT_HEAD_DIM, PAGE), F32),
                            pltpu.VMEM((ATT_HEADS, npg, ATT_HEAD_DIM, PAGE), F32),
                            pltpu.SemaphoreType.DMA((2,))],
        ),
        out_shape=jax.ShapeDtypeStruct((db, ATT_HEADS, ATT_HEAD_DIM), F32),
        compiler_params=_params(("arbitrary",)),
        name="sample_attn",
    )(page_table, idx, slopes, q, k_new, v_new, cache_kt, cache_vt)


def _sample_moba_layer(x, mod, norm_w, w_in, w_out, final_w, cache_kt, cache_vt, page_table):
    db = x.shape[0]
    shift, scale, gate = _split_mod(mod)
    row = lambda v: v.reshape(1, -1)
    proj = _norm_linear(x, row(norm_w), scale, shift, w_in.astype(BF16), tm=db, tn=1024)
    hs = (db, ATT_HEADS, ATT_HEAD_DIM)
    q, k, v = (proj[:, i * ATT_WIDTH:(i + 1) * ATT_WIDTH].reshape(hs) for i in range(3))
    g = proj[:, 3 * ATT_WIDTH:]
    idx = _sample_select(q.reshape(db, ATT_WIDTH, 1), _paged_mean(cache_kt, page_table))
    idx = idx[:, :, :MOBA_TOPK].reshape(db, ATT_HEADS * MOBA_TOPK)
    o = _sample_attn(q, k, v, cache_kt, cache_vt, page_table, idx, _alibi_slopes())
    y = _gated_out(o.reshape(db, ATT_WIDTH), g, x, gate, w_out.astype(BF16), row(final_w), tm=db)
    return y, k, v


def kernel(x_prompt, x_sample, cache_k, cache_v, state_ssm, state_conv, page_table, c_prompt, c_sample,
           ada_w, ada_b, norm_w, ssd_w_in, ssd_conv_w, ssd_conv_b, ssd_dt_bias, ssd_a_log, ssd_d,
           ssd_norm_w, ssd_w_out, att_w_in, att_w_out, final_norm_w):
    bsz, t, d = x_prompt.shape
    db = x_sample.shape[0]
    c_rows = jnp.concatenate([c_prompt, c_sample, jnp.zeros((7, d), F32)], axis=0)
    mod = _ada_mod(c_rows, ada_w, ada_b)

    xp = x_prompt[0]
    xp, ssm_p, conv_p = _prompt_ssd_layer(xp, mod[0, 0], norm_w[0], ssd_w_in[0], ssd_conv_w[0], ssd_conv_b[0],
                                          ssd_dt_bias[0], ssd_a_log[0], ssd_d[0], ssd_norm_w[0], ssd_w_out[0])
    y_p, k_p, v_p = _prompt_moba_layer(xp, mod[1, 0], norm_w[1], att_w_in[0], att_w_out[0], final_norm_w)

    xs = x_sample[:, 0]
    xs, ssm_s, conv_s = _sample_ssd_layer(xs, mod[0, 1:1 + db], norm_w[0], ssd_w_in[0], ssd_conv_w[0], ssd_conv_b[0],
                                          ssd_dt_bias[0], ssd_a_log[0], ssd_d[0], ssd_norm_w[0], ssd_w_out[0],
                                          state_conv[0], state_ssm[0])
    y_s, k_s, v_s = _sample_moba_layer(xs, mod[1, 1:1 + db], norm_w[1], att_w_in[0], att_w_out[0], final_norm_w,
                                       jnp.transpose(cache_k[0], (0, 2, 3, 1)),
                                       jnp.transpose(cache_v[0], (0, 2, 3, 1)), page_table)

    n_pg = t // PAGE
    kv_s_shape = (1, db, 1, ATT_HEADS, ATT_HEAD_DIM)

    def paged_kv(a):
        a = a.reshape(n_pg, ATT_HEADS, ATT_HEAD_DIM, PAGE)
        return jnp.transpose(a, (0, 3, 1, 2)).reshape(1, bsz, n_pg, PAGE, ATT_HEADS, ATT_HEAD_DIM)

    return (y_p.reshape(bsz, t, d),
            y_s.reshape(db, 1, d),
            paged_kv(k_p), paged_kv(v_p),
            k_s.reshape(kv_s_shape), v_s.reshape(kv_s_shape),
            ssm_p.reshape(1, bsz, SSD_HEADS, SSD_HEAD_DIM, SSD_STATE),
            conv_p.reshape(1, bsz, SSD_CONV - 1, CONV_DIM),
            ssm_s.reshape(1, db, SSD_HEADS, SSD_HEAD_DIM, SSD_STATE),
            conv_s.reshape(1, db, SSD_CONV - 1, CONV_DIM))
```

```python
import functools

import numpy as np
import jax
import jax.numpy as jnp
from jax import lax
from jax.experimental import pallas as pl
from jax.experimental.pallas import tpu as pltpu

F32 = jnp.float32
BF16 = jnp.bfloat16
HI = lax.Precision.HIGHEST

D_MODEL = 1024
D_INNER = 2048
SSD_HEADS = 32
SSD_HEAD_DIM = 64
SSD_GROUPS = 8
SSD_STATE = 128
SSD_CONV = 4
CHUNK = 128
GN = SSD_GROUPS * SSD_STATE
CONV_DIM = D_INNER + 2 * GN
ATT_HEADS = 16
ATT_HEAD_DIM = 64
ATT_WIDTH = ATT_HEADS * ATT_HEAD_DIM
MOBA_BLOCK = 256
MOBA_TOPK = 3
PAGE = 128
EPS = 1e-6
NEG = -1e30

VMEM_LIMIT = 56 * 1024 * 1024

NT = (((1,), (1,)), ((), ()))
TN = (((0,), (0,)), ((), ()))


def _silu(x):
    return x / (1.0 + jnp.exp(-x))


def _softplus(x):
    return jnp.maximum(x, 0.0) + jnp.log1p(jnp.exp(-jnp.abs(x)))


def _split3(x):
    hi = x.astype(BF16)
    r1 = x - hi.astype(F32)
    mid = r1.astype(BF16)
    lo = (r1 - mid.astype(F32)).astype(BF16)
    return hi, mid, lo


def _expand_lanes(x, e_ref):
    hi, mid, lo = _split3(x)
    e = e_ref[...]
    out = jnp.dot(hi, e, preferred_element_type=F32)
    out = out + jnp.dot(mid, e, preferred_element_type=F32)
    return out + jnp.dot(lo, e, preferred_element_type=F32)


def _dot3_nt(a, b):
    ah, am, _ = _split3(a)
    bh, bm, _ = _split3(b)
    dot = lambda x, y: lax.dot_general(x, y, NT, preferred_element_type=F32)
    return dot(ah, bh) + (dot(ah, bm) + dot(am, bh))


def _params(sem):
    return pltpu.CompilerParams(dimension_semantics=sem, vmem_limit_bytes=VMEM_LIMIT)


def _ada_kernel(c_ref, w_ref, b_ref, o_ref):
    sc = _silu(c_ref[...])
    o_ref[...] = jnp.dot(sc, w_ref[...], precision=HI, preferred_element_type=F32) + b_ref[...]


def _ada_mod(c_rows, ada_w, ada_b):
    depth, d, n3 = ada_w.shape
    m = c_rows.shape[0]
    tn = 1024
    return pl.pallas_call(
        _ada_kernel,
        grid=(depth, n3 // tn),
        in_specs=[
            pl.BlockSpec((m, d), lambda i, j: (0, 0)),
            pl.BlockSpec((None, d, tn), lambda i, j: (i, 0, j)),
            pl.BlockSpec((None, 1, tn), lambda i, j: (i, 0, j)),
        ],
        out_specs=pl.BlockSpec((None, m, tn), lambda i, j: (i, 0, j)),
        out_shape=jax.ShapeDtypeStruct((depth, m, n3), F32),
        compiler_params=_params(("arbitrary", "arbitrary")),
        name="ada_mod",
    )(c_rows, ada_w, ada_b.reshape(depth, 1, n3))


def _norm_mod(x, nw, scale, shift):
    ms = jnp.mean(x * x, axis=-1, keepdims=True)
    h = x * lax.rsqrt(ms + EPS) * nw
    return h * (1.0 + scale) + shift


def _norm_linear_kernel(x_ref, nw_ref, sc_ref, sh_ref, w_ref, o_ref):
    h = _norm_mod(x_ref[...], nw_ref[...], sc_ref[...], sh_ref[...])
    o_ref[...] = jnp.dot(h.astype(BF16), w_ref[...], preferred_element_type=F32)


def _norm_linear(x, nw, scale, shift, w_bf16, *, tm, tn):
    m, k = x.shape
    n = w_bf16.shape[1]
    per_row = scale.shape[0] != 1
    vec = (pl.BlockSpec((tm, k), lambda j, i: (i, 0)) if per_row
           else pl.BlockSpec((1, k), lambda j, i: (0, 0)))
    return pl.pallas_call(
        _norm_linear_kernel,
        grid=(n // tn, m // tm),
        in_specs=[
            pl.BlockSpec((tm, k), lambda j, i: (i, 0)),
            pl.BlockSpec((1, k), lambda j, i: (0, 0)),
            vec, vec,
            pl.BlockSpec((k, tn), lambda j, i: (0, j)),
        ],
        out_specs=pl.BlockSpec((tm, tn), lambda j, i: (i, j)),
        out_shape=jax.ShapeDtypeStruct((m, n), F32),
        compiler_params=_params(("arbitrary", "arbitrary")),
        name="norm_linear",
    )(x, nw, scale, shift, w_bf16)


VROWS = 80


def _k_proj_kernel(x_ref, nw_ref, sc_ref, sh_ref, w_ref, kt_ref, k16_ref, km_ref):
    h = _norm_mod(x_ref[...], nw_ref[...], sc_ref[...], sh_ref[...])
    acc = jnp.dot(h.astype(BF16), w_ref[...], preferred_element_type=F32)
    for pg in range(MOBA_BLOCK // PAGE):
        kt_ref[pg] = acc[pg * PAGE:(pg + 1) * PAGE, :].T
    k16_ref[...] = acc.astype(BF16)
    km_ref[...] = jnp.mean(acc, axis=0, keepdims=True)


def _v_proj_kernel(x_ref, nw_ref, sc_ref, sh_ref, w_ref, vt_ref, vh_ref):
    h = _norm_mod(x_ref[...], nw_ref[...], sc_ref[...], sh_ref[...])
    acc = jnp.dot(h.astype(BF16), w_ref[...], preferred_element_type=F32)
    acc_t = acc.T
    for pg in range(MOBA_BLOCK // PAGE):
        vt_ref[pg] = acc_t[:, pg * PAGE:(pg + 1) * PAGE]
    pad = jnp.concatenate([jnp.ones((1, MOBA_BLOCK), F32),
                           jnp.zeros((VROWS - ATT_HEAD_DIM - 1, MOBA_BLOCK), F32)], axis=0)
    for hd in range(ATT_HEADS):
        rows = acc_t[hd * ATT_HEAD_DIM:(hd + 1) * ATT_HEAD_DIM, :]
        vh_ref[hd] = jnp.concatenate([rows, pad], axis=0).astype(BF16)


def _kv_proj(x, nw, scale, shift, w_bf16, which):
    t, k = x.shape
    w = w_bf16.shape[1]
    B = MOBA_BLOCK
    nb = t // B
    ppb = B // PAGE
    one = pl.BlockSpec((1, k), lambda i: (0, 0))
    paged = (jax.ShapeDtypeStruct((t // PAGE, w, PAGE), F32), pl.BlockSpec((ppb, w, PAGE), lambda i: (i, 0, 0)))
    if which == "k":
        body = _k_proj_kernel
        outs = [paged,
                (jax.ShapeDtypeStruct((t, w), BF16), pl.BlockSpec((B, w), lambda i: (i, 0))),
                (jax.ShapeDtypeStruct((nb, 1, w), F32), pl.BlockSpec((None, 1, w), lambda i: (i, 0, 0)))]
    else:
        body = _v_proj_kernel
        outs = [paged,
                (jax.ShapeDtypeStruct((nb, ATT_HEADS, VROWS, B), BF16),
                 pl.BlockSpec((None, ATT_HEADS, VROWS, B), lambda i: (i, 0, 0, 0)))]
    return pl.pallas_call(
        body,
        grid=(nb,),
        in_specs=[pl.BlockSpec((B, k), lambda i: (i, 0)), one, one, one,
                  pl.BlockSpec((k, w), lambda i: (0, 0))],
        out_specs=[o[1] for o in outs],
        out_shape=[o[0] for o in outs],
        compiler_params=_params(("arbitrary",)),
        name=which + "_proj",
    )(x, nw, scale, shift, w_bf16)


def _gated_out_kernel(*refs, has_g, has_final):
    it = iter(refs)
    a_ref = next(it)
    g_ref = next(it) if has_g else None
    x_ref, gate_ref, w_ref = next(it), next(it), next(it)
    fw_ref = next(it) if has_final else None
    o_ref = next(it)
    a = a_ref[...].astype(F32)
    if has_g:
        a = a * _silu(g_ref[...])
    y = x_ref[...] + gate_ref[...] * jnp.dot(a.astype(BF16), w_ref[...], preferred_element_type=F32)
    if has_final:
        ms = jnp.mean(y * y, axis=-1, keepdims=True)
        y = y * lax.rsqrt(ms + EPS) * fw_ref[...]
    o_ref[...] = y


def _gated_out(a, g, x, gate, w_bf16, final_w, *, tm):
    m, k = a.shape
    n = w_bf16.shape[1]
    per_row = gate.shape[0] != 1
    row = lambda width: pl.BlockSpec((tm, width), lambda i: (i, 0))
    one = lambda width: pl.BlockSpec((1, width), lambda i: (0, 0))
    ins, specs = [a], [row(k)]
    if g is not None:
        ins.append(g)
        specs.append(row(k))
    ins += [x, gate, w_bf16]
    specs += [row(n), row(n) if per_row else one(n), pl.BlockSpec((k, n), lambda i: (0, 0))]
    if final_w is not None:
        ins.append(final_w)
        specs.append(one(n))
    return pl.pallas_call(
        functools.partial(_gated_out_kernel, has_g=g is not None, has_final=final_w is not None),
        grid=(m // tm,),
        in_specs=specs,
        out_specs=row(n),
        out_shape=jax.ShapeDtypeStruct((m, n), F32),
        compiler_params=_params(("arbitrary",)),
        name="gated_out",
    )(*ins)


def _ssd_prompt_kernel(x_ref, xbc_ref, z_ref, modv_ref, wdt_ref, wdtT_ref, hrow_ref, hcol_ref,
                       cw_ref, cb_ref, dexp_ref, snw_ref, wout_ref,
                       xo_ref, st_ref, tail_ref,
                       xpad, xg_s, b_s, c_s, xdt_s, abc_s, acst_s, y_s, gn_s):
    L = CHUNK
    ci = pl.program_id(0)
    nci = pl.num_programs(0)

    @pl.when(ci == 0)
    def _():
        xpad[0:8, :] = jnp.zeros((8, CONV_DIM), F32)
        st_ref[...] = jnp.zeros(st_ref.shape, F32)

    x = x_ref[...]
    hn = _norm_mod(x, modv_ref[0:1, :], modv_ref[1:2, :], modv_ref[2:3, :]).astype(BF16)
    dt_raw = jnp.dot(hn, wdt_ref[...], preferred_element_type=F32)
    dt_rawT = lax.dot_general(wdtT_ref[...], hn, NT, preferred_element_type=F32)
    dt = _softplus(dt_raw + hrow_ref[0:1, :])
    dtT = _softplus(dt_rawT + hcol_ref[:, 0:1])
    dA = dt * (-jnp.exp(hrow_ref[1:2, :]))
    dAT = dtT * (-jnp.exp(hcol_ref[:, 1:2]))
    r_i = lax.broadcasted_iota(jnp.int32, (L, L), 0)
    c_i = lax.broadcasted_iota(jnp.int32, (L, L), 1)
    tril = (r_i >= c_i).astype(F32)
    triu = (r_i <= c_i).astype(F32)
    acs = jnp.dot(tril, dA, precision=HI, preferred_element_type=F32)
    acsT = jnp.dot(dAT, triu, precision=HI, preferred_element_type=F32)
    for h in range(SSD_HEADS):
        acst_s[h] = acsT[h:h + 1, :]
        abc_s[h] = jnp.broadcast_to(acs[:, h:h + 1], (L, 128))
    lane = lax.broadcasted_iota(jnp.int32, (L, 128), 1)
    lo_lane = lane < SSD_HEAD_DIM

    def dt_pair(i):
        return jnp.where(lo_lane, jnp.broadcast_to(dt[:, 2 * i:2 * i + 1], (L, 128)),
                         jnp.broadcast_to(dt[:, 2 * i + 1:2 * i + 2], (L, 128)))

    xpad[8:8 + L, :] = xbc_ref[...]
    cw = 512
    for blk in range(CONV_DIM // cw):
        cs = slice(blk * cw, (blk + 1) * cw)
        acc = cb_ref[:, cs] + xpad[5:5 + L, cs] * cw_ref[0:1, cs]
        acc = acc + xpad[6:6 + L, cs] * cw_ref[1:2, cs]
        acc = acc + xpad[7:7 + L, cs] * cw_ref[2:3, cs]
        acc = acc + xpad[8:8 + L, cs] * cw_ref[3:4, cs]
        xc = _silu(acc)
        if blk < 4:
            for q in range(2):
                g = blk * 2 + q
                xg = xc[:, q * 256:(q + 1) * 256]
                xg_s[g] = xg
                xdt_s[2 * g] = xg[:, 0:128] * dt_pair(2 * g)
                xdt_s[2 * g + 1] = xg[:, 128:256] * dt_pair(2 * g + 1)
        elif blk < 6:
            for q in range(4):
                b_s[(blk - 4) * 4 + q] = xc[:, q * 128:(q + 1) * 128].astype(BF16)
        else:
            for q in range(4):
                c_s[(blk - 6) * 4 + q] = xc[:, q * 128:(q + 1) * 128]

    @pl.when(ci == nci - 1)
    def _():
        tail_ref[...] = xpad[5 + L:8 + L, :]

    xpad[5:8, :] = xpad[5 + L:8 + L, :]

    causal = r_i >= c_i
    sub = lax.broadcasted_iota(jnp.int32, (128, 128), 0)
    lo_sub = sub < 64

    def group_compute(g):
        bb = b_s[g]
        cg = c_s[g]
        cbm = lax.dot_general(cg.astype(BF16), bb, NT, preferred_element_type=F32)
        out = []
        for pr in range(2):
            pidx = 2 * g + pr
            xdt_pair = xdt_s[pidx]
            hprev = st_ref[pidx]
            hprev_b = hprev.astype(BF16)
            ypair = jnp.zeros((L, 128), F32)
            dends, cds = [], []
            for hh in range(2):
                h = 2 * pidx + hh
                abc = abc_s[h]
                arow = acst_s[h]
                dec = jnp.exp(jnp.where(causal, abc - arow, NEG))
                mh = (cbm * dec).astype(BF16)
                keep = lo_lane if hh == 0 else jnp.logical_not(lo_lane)
                xm = jnp.where(keep, xdt_pair, 0.0).astype(BF16)
                ypair = ypair + jnp.dot(mh, xm, preferred_element_type=F32)
                cd = (cg * jnp.exp(abc)).astype(BF16)
                yo = lax.dot_general(cd, hprev_b, NT, preferred_element_type=F32)
                ypair = ypair + jnp.where(keep, yo, 0.0)
                tot = abc[L - 1:L, :]
                dends.append(jnp.exp(tot - abc))
                cds.append(jnp.exp(tot))
            dend = jnp.where(lo_lane, dends[0], dends[1])
            xw = (xdt_pair * dend).astype(BF16)
            states = lax.dot_general(xw, bb, TN, preferred_element_type=F32)
            cdm = jnp.where(lo_sub, jnp.broadcast_to(cds[0], (128, 128)),
                            jnp.broadcast_to(cds[1], (128, 128)))
            out.append((pidx, ypair, hprev * cdm + states))
        return out

    def groups_body(i, carry):
        for pidx, ypair, state in group_compute(2 * i) + group_compute(2 * i + 1):
            y_s[pidx] = ypair
            st_ref[pidx] = state
        return carry

    lax.fori_loop(0, SSD_GROUPS // 2, groups_body, 0)

    for g in range(SSD_GROUPS):
        cs = slice(g * 256, (g + 1) * 256)
        yg = jnp.concatenate([y_s[2 * g], y_s[2 * g + 1]], axis=1)
        gg = (yg + xg_s[g] * dexp_ref[:, cs]) * _silu(z_ref[:, cs])
        ms = jnp.mean(gg * gg, axis=-1, keepdims=True)
        gn_s[:, cs] = (gg * lax.rsqrt(ms + EPS) * snw_ref[:, cs]).astype(BF16)

    xo_ref[...] = x + modv_ref[3:4, :] * jnp.dot(gn_s[...], wout_ref[...], preferred_element_type=F32)


def _ssd_prompt(x, proj, modv, wdt, wdtT, hrow, hcol, conv_w, conv_b, dexp, snw, wout):
    t = x.shape[0]
    nc = t // CHUNK
    L = CHUNK
    full = lambda a: pl.BlockSpec(a.shape, lambda c: (0,) * a.ndim)
    return pl.pallas_call(
        _ssd_prompt_kernel,
        grid=(nc,),
        in_specs=[
            pl.BlockSpec((L, D_MODEL), lambda c: (c, 0)),
            pl.BlockSpec((L, CONV_DIM), lambda c: (c, 0)),
            pl.BlockSpec((L, D_INNER), lambda c: (c, 2)),
            full(modv), full(wdt), full(wdtT), full(hrow), full(hcol), full(conv_w), full(conv_b),
            full(dexp), full(snw), full(wout),
        ],
        out_specs=[
            pl.BlockSpec((L, D_MODEL), lambda c: (c, 0)),
            pl.BlockSpec((SSD_HEADS // 2, 128, SSD_STATE), lambda c: (0, 0, 0)),
            pl.BlockSpec((SSD_CONV - 1, CONV_DIM), lambda c: (0, 0)),
        ],
        out_shape=[
            jax.ShapeDtypeStruct((t, D_MODEL), F32),
            jax.ShapeDtypeStruct((SSD_HEADS // 2, 128, SSD_STATE), F32),
            jax.ShapeDtypeStruct((SSD_CONV - 1, CONV_DIM), F32),
        ],
        scratch_shapes=[
            pltpu.VMEM((8 + L, CONV_DIM), F32),
            pltpu.VMEM((SSD_GROUPS, L, 256), F32),
            pltpu.VMEM((SSD_GROUPS, L, SSD_STATE), BF16),
            pltpu.VMEM((SSD_GROUPS, L, SSD_STATE), F32),
            pltpu.VMEM((SSD_HEADS // 2, L, 128), F32),
            pltpu.VMEM((SSD_HEADS, L, 128), F32),
            pltpu.VMEM((SSD_HEADS, 1, L), F32),
            pltpu.VMEM((SSD_HEADS // 2, L, 128), F32),
            pltpu.VMEM((L, D_INNER), BF16),
        ],
        compiler_params=_params(("arbitrary",)),
        name="ssd_prompt",
    )(x, proj, proj, modv, wdt, wdtT, hrow, hcol, conv_w, conv_b, dexp, snw, wout)


def _expansion(heads, width):
    e = np.zeros((heads, heads * width), np.float32)
    for h in range(heads):
        e[h, h * width:(h + 1) * width] = 1.0
    return jnp.asarray(e, BF16)


def _split_mod(mod):
    return mod[..., :D_MODEL], mod[..., D_MODEL:2 * D_MODEL], mod[..., 2 * D_MODEL:]


def _prompt_ssd_layer(x, mod, norm_w, w_in, conv_w, conv_b, dt_bias, a_log, d_skip, ssd_norm_w, w_out):
    shift, scale, gate = _split_mod(mod)
    row = lambda v: v.reshape(1, -1)
    w_main = jnp.concatenate([w_in[:, D_INNER:D_INNER + CONV_DIM], w_in[:, :D_INNER]], axis=1).astype(BF16)
    w_dt = w_in[:, D_INNER + CONV_DIM:].astype(BF16)
    proj = _norm_linear(x, row(norm_w), row(scale), row(shift), w_main, tm=512, tn=2048)
    zeros = jnp.zeros((4, D_MODEL), F32)
    modv = jnp.concatenate([row(norm_w), row(scale), row(shift), row(gate), zeros], axis=0)
    hrow = jnp.concatenate([row(dt_bias), row(a_log), jnp.zeros((6, SSD_HEADS), F32)], axis=0)
    hcol = hrow.T
    dexp = jnp.repeat(d_skip, SSD_HEAD_DIM).reshape(1, D_INNER)
    return _ssd_prompt(x, proj, modv, w_dt, w_dt.T, hrow, hcol, conv_w, row(conv_b), dexp,
                       row(ssd_norm_w), w_out.astype(BF16))


def _topk_block_mask(gate, blk, n_valid):
    nb = gate.shape[0]
    ninf = jnp.float32(-jnp.inf)
    blk_f = blk.astype(F32)
    g = jnp.where(blk < n_valid, gate, ninf)
    sel = jnp.zeros(gate.shape, jnp.bool_)
    for _ in range(MOBA_TOPK):
        m = jnp.max(g, axis=0, keepdims=True)
        first = jnp.min(jnp.where(g == m, blk_f, float(nb)), axis=0, keepdims=True)
        pick = jnp.logical_and(blk_f == first, m > ninf)
        sel = jnp.logical_or(sel, pick)
        g = jnp.where(pick, ninf, g)
    return jnp.where(sel, 0.0, NEG)


def _moba_select_kernel(q_ref, km_ref, o_ref):
    c = pl.program_id(0)
    nb = km_ref.shape[0]
    tq = q_ref.shape[0]
    lane = lax.broadcasted_iota(jnp.int32, (tq, 128), 1)
    blk = lax.broadcasted_iota(jnp.int32, (nb, tq), 0)
    for p in range(ATT_HEADS // 2):
        qp = q_ref[:, p * 128:(p + 1) * 128]
        kmp = km_ref[:, p * 128:(p + 1) * 128]
        for hh in range(2):
            keep = (lane < 64) if hh == 0 else (lane >= 64)
            qm = jnp.where(keep, qp, 0.0)
            gate = _dot3_nt(kmp, qm)
            o_ref[2 * p + hh] = _topk_block_mask(gate, blk, c)


def _moba_select(q, kmean):
    t, w = q.shape
    nb = kmean.shape[0]
    return pl.pallas_call(
        _moba_select_kernel,
        grid=(nb,),
        in_specs=[pl.BlockSpec((MOBA_BLOCK, w), lambda c: (c, 0)),
                  pl.BlockSpec((nb, w), lambda c: (0, 0))],
        out_specs=pl.BlockSpec((ATT_HEADS, nb, MOBA_BLOCK), lambda c: (0, 0, c)),
        out_shape=jax.ShapeDtypeStruct((ATT_HEADS, nb, t), F32),
        compiler_params=_params(("arbitrary",)),
        name="moba_select",
    )(q, kmean)


LOG2E = 1.4426950408889634


def _moba_attn_kernel(slopes_ref, q_ref, k_ref, vh_ref, mask_ref, o_ref, spa_s, spb_s):
    B = MOBA_BLOCK
    hd = ATT_HEAD_DIM
    p = pl.program_id(0)
    c = pl.program_id(1)
    q = q_ref[...] * (hd ** -0.5 * LOG2E)
    lane = lax.broadcasted_iota(jnp.int32, (B, 128), 1)
    qh = [jnp.where(lane < hd, q, 0.0).astype(BF16), jnp.where(lane >= hd, q, 0.0).astype(BF16)]
    s_i = lax.broadcasted_iota(jnp.int32, (B, B), 0)
    t_i = lax.broadcasted_iota(jnp.int32, (B, B), 1)
    dloc = (s_i - t_i).astype(F32)
    sl = [slopes_ref[2 * p] * LOG2E, slopes_ref[2 * p + 1] * LOG2E]
    bh = [dloc * sl[0], dloc * sl[1]]

    def scores(j, hh):
        return lax.dot_general(k_ref[j], qh[hh], NT, preferred_element_type=F32) + bh[hh]

    sps = [jnp.where(s_i <= t_i, scores(c, hh), NEG) for hh in range(2)]
    ms = [jnp.max(sp, axis=0, keepdims=True) for sp in sps]
    pes = [jnp.exp2(sp - m).astype(BF16) for sp, m in zip(sps, ms)]
    accs = [jnp.dot(vh_ref[c, hh], pes[hh], preferred_element_type=F32) for hh in range(2)]

    combos = [(hh, n) for hh in range(2) for n in range(2)]

    def trip_blocks(t):
        return [jnp.minimum(2 * t + n, c) for n in range(2)]

    def put_scores(t, buf):
        js = trip_blocks(t)
        for k, (hh, n) in enumerate(combos):
            buf[k] = scores(js[n], hh)

    def softmax_pv(t, buf, carry):
        m_old, acc_old = carry[:2], carry[2:]
        js = trip_blocks(t)
        radd = {(hh, n): mask_ref[hh, pl.ds(js[n], 1), :] - (c - js[n]).astype(F32) * float(B) * sl[hh]
                for hh, n in combos}
        mx = {cm: jnp.max(buf[k], axis=0, keepdims=True) + radd[cm] for k, cm in enumerate(combos)}
        m_new = [jnp.maximum(m_old[hh], jnp.maximum(mx[(hh, 0)], mx[(hh, 1)])) for hh in range(2)]
        pe = {(hh, n): jnp.exp2(buf[k] - (m_new[hh] - radd[(hh, n)])).astype(BF16)
              for k, (hh, n) in enumerate(combos)}
        pv = {(hh, n): jnp.dot(vh_ref[js[n], hh], pe[(hh, n)], preferred_element_type=F32) for hh, n in combos}
        acc_new = [acc_old[hh] * jnp.exp2(m_old[hh] - m_new[hh]) + (pv[(hh, 0)] + pv[(hh, 1)])
                   for hh in range(2)]
        return tuple(m_new) + tuple(acc_new)

    put_scores(0, spa_s)

    def body(i, carry):
        put_scores(2 * i + 1, spb_s)
        carry = softmax_pv(2 * i, spa_s, carry)
        put_scores(2 * i + 2, spa_s)
        return softmax_pv(2 * i + 1, spb_s, carry)

    res = lax.fori_loop(0, lax.shift_right_logical(c + 3, 2), body, tuple(ms) + tuple(accs))
    outs = [acc[0:hd, :] / acc[hd:hd + 1, :] for acc in res[2:4]]
    o_ref[...] = jnp.concatenate(outs, axis=0).T


def _moba_attn(q, k_blocks, vh_blocks, mask, slopes):
    t, w = q.shape
    nb = k_blocks.shape[0]
    B = MOBA_BLOCK
    return pl.pallas_call(
        _moba_attn_kernel,
        grid=(ATT_HEADS // 2, nb),
        in_specs=[
            pl.BlockSpec(memory_space=pltpu.SMEM),
            pl.BlockSpec((B, 128), lambda p, c: (c, p)),
            pl.BlockSpec((nb, B, 128), lambda p, c: (0, 0, p)),
            pl.BlockSpec((nb, 2, VROWS, B), lambda p, c: (0, p, 0, 0)),
            pl.BlockSpec((2, nb, B), lambda p, c: (p, 0, c)),
        ],
        out_specs=pl.BlockSpec((B, 128), lambda p, c: (c, p)),
        out_shape=jax.ShapeDtypeStruct((t, w), F32),
        scratch_shapes=[pltpu.VMEM((4, B, B), F32), pltpu.VMEM((4, B, B), F32)],
        compiler_params=_params(("arbitrary", "arbitrary")),
        name="moba_attn",
    )(slopes, q, k_blocks, vh_blocks, mask)


def _alibi_slopes():
    return jnp.asarray(2.0 ** (-8.0 * np.arange(1, ATT_HEADS + 1) / ATT_HEADS), dtype=F32)


def _prompt_moba_layer(x, mod, norm_w, w_in, w_out, final_w):
    shift, scale, gate = _split_mod(mod)
    row = lambda v: v.reshape(1, -1)
    nw, sc, sh = row(norm_w), row(scale), row(shift)
    wq, wk, wv, wg = (w_in[:, i * ATT_WIDTH:(i + 1) * ATT_WIDTH].astype(BF16) for i in range(4))
    B = MOBA_BLOCK
    nb = x.shape[0] // B
    q = _norm_linear(x, nw, sc, sh, wq, tm=512, tn=1024)
    g = _norm_linear(x, nw, sc, sh, wg, tm=512, tn=1024)
    kt, k16, kmean = _kv_proj(x, nw, sc, sh, wk, "k")
    vt, vh16 = _kv_proj(x, nw, sc, sh, wv, "v")
    mask = _moba_select(q, kmean.reshape(nb, ATT_WIDTH))
    o = _moba_attn(q, k16.reshape(nb, B, ATT_WIDTH), vh16, mask, _alibi_slopes())
    y = _gated_out(o, g, x, row(gate), w_out.astype(BF16), row(final_w), tm=512)
    return y, kt, vt


def _ssd_step_kernel(xbc_ref, z_ref, dtr_ref, cst_ref, st_ref, hrow_ref, cw_ref, cb_ref, dexp_ref, snw_ref,
                     e64_ref, sel_ref, gn_ref, sto_ref, cso_ref):
    xbc = xbc_ref[...]
    conv = cb_ref[...] + cst_ref[0:1, :] * cw_ref[0:1, :]
    conv = conv + cst_ref[1:2, :] * cw_ref[1:2, :]
    conv = conv + cst_ref[2:3, :] * cw_ref[2:3, :]
    conv = conv + xbc * cw_ref[3:4, :]
    cso_ref[0:2, :] = cst_ref[1:3, :]
    cso_ref[2:3, :] = xbc
    xc = _silu(conv)
    xs = xc[:, :D_INNER]
    dt = _softplus(dtr_ref[:, :SSD_HEADS] + hrow_ref[0:1, :])
    dec = jnp.exp(dt * (-jnp.exp(hrow_ref[1:2, :])))
    both = _expand_lanes(jnp.concatenate([dt, dec, jnp.zeros((6, SSD_HEADS), F32)], axis=0), e64_ref)
    xdt = xs * both[0:1, :]
    dec_exp = both[1:2, :]
    pieces = [p.astype(F32) for p in _split3(xdt) + _split3(dec_exp)]
    stack = jnp.concatenate(pieces + [jnp.zeros((10, D_INNER), F32)], axis=0).astype(BF16)
    cols = lax.dot_general(stack, sel_ref[...], TN, preferred_element_type=F32)
    ys = []
    for i in range(SSD_HEADS // 2):
        g = i // 2
        brow = xc[:, D_INNER + g * 128:D_INNER + (g + 1) * 128]
        crow = xc[:, D_INNER + GN + g * 128:D_INNER + GN + (g + 1) * 128]
        xcol = cols[i * 128:(i + 1) * 128, 0:1]
        dcol = cols[i * 128:(i + 1) * 128, 1:2]
        hn = st_ref[i] * dcol + xcol * brow
        sto_ref[i] = hn
        c8 = jnp.broadcast_to(crow, (8, SSD_STATE)).astype(BF16)
        ys.append(lax.dot_general(c8, hn.astype(BF16), NT, preferred_element_type=F32)[0:1, :])
    y = jnp.concatenate(ys, axis=1)
    gg = (y + xs * dexp_ref[...]) * _silu(z_ref[...])
    outs = []
    for g in range(SSD_GROUPS):
        gs = gg[:, g * 256:(g + 1) * 256]
        ms = jnp.mean(gs * gs, axis=-1, keepdims=True)
        outs.append(gs * lax.rsqrt(ms + EPS))
    gn_ref[...] = jnp.concatenate(outs, axis=1) * snw_ref[...]


def _ssd_step(proj, dt_raw, conv_state, ssm_state, hrow, conv_w, conv_b, dexp, snw, e64):
    db = proj.shape[0]
    sel = np.zeros((16, 128), np.float32)
    sel[0:3, 0] = 1.0
    sel[3:6, 1] = 1.0
    sel = jnp.asarray(sel, BF16)
    full = lambda a: pl.BlockSpec(a.shape, lambda b: (0,) * a.ndim)
    hp = SSD_HEADS // 2
    return pl.pallas_call(
        _ssd_step_kernel,
        grid=(db,),
        in_specs=[
            pl.BlockSpec((None, 1, CONV_DIM), lambda b: (b, 0, 0)),
            pl.BlockSpec((None, 1, D_INNER), lambda b: (b, 0, 2)),
            pl.BlockSpec((None, 1, 128), lambda b: (b, 0, 0)),
            pl.BlockSpec((None, SSD_CONV - 1, CONV_DIM), lambda b: (b, 0, 0)),
            pl.BlockSpec((None, hp, 128, SSD_STATE), lambda b: (b, 0, 0, 0)),
            full(hrow), full(conv_w), full(conv_b), full(dexp), full(snw), full(e64), full(sel),
        ],
        out_specs=[
            pl.BlockSpec((None, 1, D_INNER), lambda b: (b, 0, 0)),
            pl.BlockSpec((None, hp, 128, SSD_STATE), lambda b: (b, 0, 0, 0)),
            pl.BlockSpec((None, SSD_CONV - 1, CONV_DIM), lambda b: (b, 0, 0)),
        ],
        out_shape=[
            jax.ShapeDtypeStruct((db, 1, D_INNER), F32),
            jax.ShapeDtypeStruct((db, hp, 128, SSD_STATE), F32),
            jax.ShapeDtypeStruct((db, SSD_CONV - 1, CONV_DIM), F32),
        ],
        compiler_params=_params(("arbitrary",)),
        name="ssd_step",
    )(proj, proj, dt_raw, conv_state, ssm_state, hrow, conv_w, conv_b, dexp, snw, e64, sel)


def _sample_ssd_layer(x, mod, norm_w, w_in, conv_w, conv_b, dt_bias, a_log, d_skip, ssd_norm_w, w_out,
                      conv_state, ssm_state):
    db = x.shape[0]
    shift, scale, gate = _split_mod(mod)
    row = lambda v: v.reshape(1, -1)
    w_main = jnp.concatenate([w_in[:, D_INNER:D_INNER + CONV_DIM], w_in[:, :D_INNER]], axis=1).astype(BF16)
    w_dt = jnp.pad(w_in[:, D_INNER + CONV_DIM:], ((0, 0), (0, 128 - SSD_HEADS))).astype(BF16)
    proj = _norm_linear(x, row(norm_w), scale, shift, w_main, tm=db, tn=2048)
    dt_raw = _norm_linear(x, row(norm_w), scale, shift, w_dt, tm=db, tn=128)
    hrow = jnp.concatenate([row(dt_bias), row(a_log), jnp.zeros((6, SSD_HEADS), F32)], axis=0)
    dexp = jnp.repeat(d_skip, SSD_HEAD_DIM).reshape(1, D_INNER)
    gn, st, cs = _ssd_step(proj.reshape(db, 1, -1), dt_raw.reshape(db, 1, 128), conv_state,
                           ssm_state.reshape(db, SSD_HEADS // 2, 128, SSD_STATE), hrow, conv_w, row(conv_b),
                           dexp, row(ssd_norm_w), _expansion(SSD_HEADS, SSD_HEAD_DIM))
    x1 = _gated_out(gn.reshape(db, D_INNER), None, x, gate, w_out.astype(BF16), None, tm=db)
    return x1, st, cs


PAGES_PER_STEP = 16
PAGES_PER_BLOCK = MOBA_BLOCK // PAGE


def _paged_mean_kernel(pt_ref, *refs):
    pages, o_ref = refs[:PAGES_PER_STEP], refs[PAGES_PER_STEP]
    s = pl.program_id(1)
    w, nb = o_ref.shape
    bps = PAGES_PER_STEP // PAGES_PER_BLOCK

    @pl.when(s == 0)
    def _():
        o_ref[...] = jnp.zeros((w, nb), F32)

    lane = lax.broadcasted_iota(jnp.int32, (w, nb), 1)
    cur = o_ref[...]
    for r in range(bps):
        tot = pages[2 * r][...].reshape(w, PAGE) + pages[2 * r + 1][...].reshape(w, PAGE)
        col = jnp.sum(tot, axis=-1, keepdims=True) * (1.0 / MOBA_BLOCK)
        cur = jnp.where(lane == s * bps + r, col, cur)
    o_ref[...] = cur


def _paged_mean(cache_t, page_table):
    db, n_pages = page_table.shape
    nsteps = n_pages // PAGES_PER_STEP
    nblk = n_pages // PAGES_PER_BLOCK

    def page_spec(r):
        return pl.BlockSpec((None, ATT_HEADS, ATT_HEAD_DIM, PAGE),
                            lambda b, s, pt: (pt[b, s * PAGES_PER_STEP + r], 0, 0, 0))

    return pl.pallas_call(
        _paged_mean_kernel,
        grid_spec=pltpu.PrefetchScalarGridSpec(
            num_scalar_prefetch=1,
            grid=(db, nsteps),
            in_specs=[page_spec(r) for r in range(PAGES_PER_STEP)],
            out_specs=pl.BlockSpec((None, ATT_WIDTH, nblk), lambda b, s, pt: (b, 0, 0)),
        ),
        out_shape=jax.ShapeDtypeStruct((db, ATT_WIDTH, nblk), F32),
        compiler_params=_params(("arbitrary", "arbitrary")),
        name="paged_mean",
    )(page_table, *([cache_t] * PAGES_PER_STEP))


def _sample_select_kernel(q_ref, km_ref, o_ref):
    w, nb = km_ref.shape
    prod = km_ref[...] * q_ref[...]
    gate = jnp.sum(prod.reshape(ATT_HEADS, ATT_HEAD_DIM, nb), axis=1)
    blk_f = lax.broadcasted_iota(jnp.int32, gate.shape, 1).astype(F32)
    lane = lax.broadcasted_iota(jnp.int32, (ATT_HEADS, 128), 1)
    ninf = jnp.float32(-jnp.inf)
    g = gate
    out = jnp.zeros((ATT_HEADS, 128), F32)
    for r in range(MOBA_TOPK):
        m = jnp.max(g, axis=-1, keepdims=True)
        first = jnp.min(jnp.where(g == m, blk_f, float(nb)), axis=-1, keepdims=True)
        out = jnp.where(lane == r, first, out)
        g = jnp.where(blk_f == first, ninf, g)
    o_ref[...] = out.astype(jnp.int32)


def _sample_select(q_col, kmean_t):
    db, w, nb = kmean_t.shape
    return pl.pallas_call(
        _sample_select_kernel,
        grid=(db,),
        in_specs=[pl.BlockSpec((None, w, 1), lambda b: (b, 0, 0)),
                  pl.BlockSpec((None, w, nb), lambda b: (b, 0, 0))],
        out_specs=pl.BlockSpec((None, ATT_HEADS, 128), lambda b: (b, 0, 0)),
        out_shape=jax.ShapeDtypeStruct((db, ATT_HEADS, 128), jnp.int32),
        compiler_params=_params(("arbitrary",)),
        name="sample_select",
    )(q_col, kmean_t)


def _sel_copies(pt_ref, idx_ref, cache_hbm, buf, sem, b):
    out = []
    for h in range(ATT_HEADS):
        for r in range(MOBA_TOPK):
            blk = idx_ref[b, h * MOBA_TOPK + r]
            for half in range(PAGES_PER_BLOCK):
                page = pt_ref[b, blk * PAGES_PER_BLOCK + half]
                out.append(pltpu.make_async_copy(cache_hbm.at[page, h],
                                                 buf.at[h, r * PAGES_PER_BLOCK + half], sem))
    return out


def _sample_attn_kernel(pt_ref, idx_ref, q_ref, kn_ref, vn_ref, idxv_ref, slopes_ref, ck_hbm, cv_hbm, o_ref,
                        kbuf, vbuf, sems, *, past_len):
    b = pl.program_id(0)
    kcopies = _sel_copies(pt_ref, idx_ref, ck_hbm, kbuf, sems.at[0], b)
    vcopies = _sel_copies(pt_ref, idx_ref, cv_hbm, vbuf, sems.at[1], b)
    for cp in kcopies + vcopies:
        cp.start()
    for cp in kcopies + vcopies:
        cp.wait()
    nsel = MOBA_TOPK * MOBA_BLOCK
    npg = MOBA_TOPK * PAGES_PER_BLOCK
    scale = ATT_HEAD_DIM ** -0.5
    q = q_ref[...]
    rows = []
    for h in range(ATT_HEADS):
        q8 = jnp.broadcast_to(q[h:h + 1, :], (8, ATT_HEAD_DIM)).astype(BF16)
        kcat = jnp.concatenate([kbuf[h, i] for i in range(npg)], axis=1).astype(BF16)
        rows.append(jnp.dot(q8, kcat, preferred_element_type=F32)[0:1, :])
    s = jnp.concatenate(rows, axis=0) * scale
    local = lax.broadcasted_iota(jnp.int32, (ATT_HEADS, nsel), 1)
    within = jnp.bitwise_and(local, MOBA_BLOCK - 1).astype(F32)
    which = jnp.right_shift(local, MOBA_BLOCK.bit_length() - 1)
    starts = (idxv_ref[...] * MOBA_BLOCK).astype(F32)
    start = jnp.zeros((ATT_HEADS, nsel), F32)
    for r in range(MOBA_TOPK):
        start = jnp.where(which == r, starts[:, r:r + 1], start)
    slope = slopes_ref[:, 0:1]
    s = s - slope * (float(past_len) - (start + within))
    s_self = jnp.sum(q.astype(BF16).astype(F32) * kn_ref[...].astype(BF16).astype(F32),
                     axis=-1, keepdims=True) * scale
    m = jnp.maximum(jnp.max(s, axis=-1, keepdims=True), s_self)
    pe = jnp.exp(s - m)
    p_self = jnp.exp(s_self - m)
    l = jnp.sum(pe, axis=-1, keepdims=True) + p_self
    pe16 = pe.astype(BF16)
    rows = []
    for h in range(ATT_HEADS):
        vcat = jnp.concatenate([vbuf[h, i] for i in range(npg)], axis=1).astype(BF16)
        p8 = jnp.broadcast_to(pe16[h:h + 1, :], (8, nsel))
        rows.append(lax.dot_general(p8, vcat, NT, preferred_element_type=F32)[0:1, :])
    o_ref[...] = (jnp.concatenate(rows, axis=0) + p_self * vn_ref[...]) / l


def _sample_attn(q, k_new, v_new, cache_kt, cache_vt, page_table, idx, idx_lanes, slopes):
    db, n_pages = page_table.shape
    npg = MOBA_TOPK * PAGES_PER_BLOCK
    vec = pl.BlockSpec((None, ATT_HEADS, ATT_HEAD_DIM), lambda b, pt, ix: (b, 0, 0))
    slopes = jnp.broadcast_to(slopes[:, None], (ATT_HEADS, 128))
    return pl.pallas_call(
        functools.partial(_sample_attn_kernel, past_len=n_pages * PAGE),
        grid_spec=pltpu.PrefetchScalarGridSpec(
            num_scalar_prefetch=2,
            grid=(db,),
            in_specs=[vec, vec, vec,
                      pl.BlockSpec((None, ATT_HEADS, 128), lambda b, pt, ix: (b, 0, 0)),
                      pl.BlockSpec((ATT_HEADS, 128), lambda b, pt, ix: (0, 0)),
                      pl.BlockSpec(memory_space=pl.ANY), pl.BlockSpec(memory_space=pl.ANY)],
            out_specs=vec,
            scratch_shapes=[pltpu.VMEM((ATT_HEADS, npg, ATT_HEAD_DIM, PAGE), F32),
                            pltpu.VMEM((ATT_HEADS, npg, ATT_HEAD_DIM, PAGE), F32),
                            pltpu.SemaphoreType.DMA((2,))],
        ),
        out_shape=jax.ShapeDtypeStruct((db, ATT_HEADS, ATT_HEAD_DIM), F32),
        compiler_params=_params(("arbitrary",)),
        name="sample_attn",
    )(page_table, idx, q, k_new, v_new, idx_lanes, slopes, cache_kt, cache_vt)


def _sample_moba_layer(x, mod, norm_w, w_in, w_out, final_w, cache_kt, cache_vt, page_table):
    db = x.shape[0]
    shift, scale, gate = _split_mod(mod)
    row = lambda v: v.reshape(1, -1)
    proj = _norm_linear(x, row(norm_w), scale, shift, w_in.astype(BF16), tm=db, tn=1024)
    hs = (db, ATT_HEADS, ATT_HEAD_DIM)
    q, k, v = (proj[:, i * ATT_WIDTH:(i + 1) * ATT_WIDTH].reshape(hs) for i in range(3))
    g = proj[:, 3 * ATT_WIDTH:]
    idx_lanes = _sample_select(q.reshape(db, ATT_WIDTH, 1), _paged_mean(cache_kt, page_table))
    idx = idx_lanes[:, :, :MOBA_TOPK].reshape(db, ATT_HEADS * MOBA_TOPK)
    o = _sample_attn(q, k, v, cache_kt, cache_vt, page_table, idx, idx_lanes, _alibi_slopes())
    y = _gated_out(o.reshape(db, ATT_WIDTH), g, x, gate, w_out.astype(BF16), row(final_w), tm=db)
    return y, k, v


def kernel(x_prompt, x_sample, cache_k, cache_v, state_ssm, state_conv, page_table, c_prompt, c_sample,
           ada_w, ada_b, norm_w, ssd_w_in, ssd_conv_w, ssd_conv_b, ssd_dt_bias, ssd_a_log, ssd_d,
           ssd_norm_w, ssd_w_out, att_w_in, att_w_out, final_norm_w):
    bsz, t, d = x_prompt.shape
    db = x_sample.shape[0]
    c_rows = jnp.concatenate([c_prompt, c_sample, jnp.zeros((7, d), F32)], axis=0)
    mod = _ada_mod(c_rows, ada_w, ada_b)

    xp = x_prompt[0]
    xp, ssm_p, conv_p = _prompt_ssd_layer(xp, mod[0, 0], norm_w[0], ssd_w_in[0], ssd_conv_w[0], ssd_conv_b[0],
                                          ssd_dt_bias[0], ssd_a_log[0], ssd_d[0], ssd_norm_w[0], ssd_w_out[0])
    y_p, k_p, v_p = _prompt_moba_layer(xp, mod[1, 0], norm_w[1], att_w_in[0], att_w_out[0], final_norm_w)

    xs = x_sample[:, 0]
    xs, ssm_s, conv_s = _sample_ssd_layer(xs, mod[0, 1:1 + db], norm_w[0], ssd_w_in[0], ssd_conv_w[0], ssd_conv_b[0],
                                          ssd_dt_bias[0], ssd_a_log[0], ssd_d[0], ssd_norm_w[0], ssd_w_out[0],
                                          state_conv[0], state_ssm[0])
    y_s, k_s, v_s = _sample_moba_layer(xs, mod[1, 1:1 + db], norm_w[1], att_w_in[0], att_w_out[0], final_norm_w,
                                       jnp.transpose(cache_k[0], (0, 2, 3, 1)),
                                       jnp.transpose(cache_v[0], (0, 2, 3, 1)), page_table)

    n_pg = t // PAGE
    kv_s_shape = (1, db, 1, ATT_HEADS, ATT_HEAD_DIM)

    def paged_kv(a):
        a = a.reshape(n_pg, ATT_HEADS, ATT_HEAD_DIM, PAGE)
        return jnp.transpose(a, (0, 3, 1, 2)).reshape(1, bsz, n_pg, PAGE, ATT_HEADS, ATT_HEAD_DIM)

    return (y_p.reshape(bsz, t, d),
            y_s.reshape(db, 1, d),
            paged_kv(k_p), paged_kv(v_p),
            k_s.reshape(kv_s_shape), v_s.reshape(kv_s_shape),
            ssm_p.reshape(1, bsz, SSD_HEADS, SSD_HEAD_DIM, SSD_STATE),
            conv_p.reshape(1, bsz, SSD_CONV - 1, CONV_DIM),
            ssm_s.reshape(1, db, SSD_HEADS, SSD_HEAD_DIM, SSD_STATE),
            conv_s.reshape(1, db, SSD_CONV - 1, CONV_DIM))
```

```python
import functools

import numpy as np
import jax
import jax.numpy as jnp
from jax import lax
from jax.experimental import pallas as pl
from jax.experimental.pallas import tpu as pltpu

F32 = jnp.float32
BF16 = jnp.bfloat16
HI = lax.Precision.HIGHEST

D_MODEL = 1024
D_INNER = 2048
SSD_HEADS = 32
SSD_HEAD_DIM = 64
SSD_GROUPS = 8
SSD_STATE = 128
SSD_CONV = 4
CHUNK = 128
GN = SSD_GROUPS * SSD_STATE
CONV_DIM = D_INNER + 2 * GN
ATT_HEADS = 16
ATT_HEAD_DIM = 64
ATT_WIDTH = ATT_HEADS * ATT_HEAD_DIM
MOBA_BLOCK = 256
MOBA_TOPK = 3
PAGE = 128
EPS = 1e-6
NEG = -1e30

VMEM_LIMIT = 56 * 1024 * 1024

NT = (((1,), (1,)), ((), ()))
TN = (((0,), (0,)), ((), ()))


def _silu(x):
    return x / (1.0 + jnp.exp(-x))


def _softplus(x):
    return jnp.maximum(x, 0.0) + jnp.log1p(jnp.exp(-jnp.abs(x)))


def _split3(x):
    hi = x.astype(BF16)
    r1 = x - hi.astype(F32)
    mid = r1.astype(BF16)
    lo = (r1 - mid.astype(F32)).astype(BF16)
    return hi, mid, lo


def _expand_lanes(x, e_ref):
    hi, mid, lo = _split3(x)
    e = e_ref[...]
    out = jnp.dot(hi, e, preferred_element_type=F32)
    out = out + jnp.dot(mid, e, preferred_element_type=F32)
    return out + jnp.dot(lo, e, preferred_element_type=F32)


def _dot3_nt(a, b):
    ah, am, _ = _split3(a)
    bh, bm, _ = _split3(b)
    dot = lambda x, y: lax.dot_general(x, y, NT, preferred_element_type=F32)
    return dot(ah, bh) + (dot(ah, bm) + dot(am, bh))


def _params(sem):
    return pltpu.CompilerParams(dimension_semantics=sem, vmem_limit_bytes=VMEM_LIMIT)


def _ada_kernel(c_ref, w_ref, b_ref, o_ref):
    sc = _silu(c_ref[...])
    o_ref[...] = jnp.dot(sc, w_ref[...], precision=HI, preferred_element_type=F32) + b_ref[...]


def _ada_mod(c_rows, ada_w, ada_b):
    depth, d, n3 = ada_w.shape
    m = c_rows.shape[0]
    tn = 1024
    return pl.pallas_call(
        _ada_kernel,
        grid=(depth, n3 // tn),
        in_specs=[
            pl.BlockSpec((m, d), lambda i, j: (0, 0)),
            pl.BlockSpec((None, d, tn), lambda i, j: (i, 0, j)),
            pl.BlockSpec((None, 1, tn), lambda i, j: (i, 0, j)),
        ],
        out_specs=pl.BlockSpec((None, m, tn), lambda i, j: (i, 0, j)),
        out_shape=jax.ShapeDtypeStruct((depth, m, n3), F32),
        compiler_params=_params(("arbitrary", "arbitrary")),
        name="ada_mod",
    )(c_rows, ada_w, ada_b.reshape(depth, 1, n3))


def _norm_mod(x, nw, scale, shift):
    ms = jnp.mean(x * x, axis=-1, keepdims=True)
    h = x * lax.rsqrt(ms + EPS) * nw
    return h * (1.0 + scale) + shift


def _norm_linear_kernel(x_ref, nw_ref, sc_ref, sh_ref, w_ref, o_ref):
    h = _norm_mod(x_ref[...], nw_ref[...], sc_ref[...], sh_ref[...])
    o_ref[...] = jnp.dot(h.astype(BF16), w_ref[...], preferred_element_type=F32)


def _norm_linear(x, nw, scale, shift, w_bf16, *, tm, tn):
    m, k = x.shape
    n = w_bf16.shape[1]
    per_row = scale.shape[0] != 1
    vec = (pl.BlockSpec((tm, k), lambda j, i: (i, 0)) if per_row
           else pl.BlockSpec((1, k), lambda j, i: (0, 0)))
    return pl.pallas_call(
        _norm_linear_kernel,
        grid=(n // tn, m // tm),
        in_specs=[
            pl.BlockSpec((tm, k), lambda j, i: (i, 0)),
            pl.BlockSpec((1, k), lambda j, i: (0, 0)),
            vec, vec,
            pl.BlockSpec((k, tn), lambda j, i: (0, j)),
        ],
        out_specs=pl.BlockSpec((tm, tn), lambda j, i: (i, j)),
        out_shape=jax.ShapeDtypeStruct((m, n), F32),
        compiler_params=_params(("arbitrary", "arbitrary")),
        name="norm_linear",
    )(x, nw, scale, shift, w_bf16)


VROWS = 80


def _k_proj_kernel(x_ref, nw_ref, sc_ref, sh_ref, w_ref, kt_ref, k16_ref, km_ref):
    h = _norm_mod(x_ref[...], nw_ref[...], sc_ref[...], sh_ref[...])
    acc = jnp.dot(h.astype(BF16), w_ref[...], preferred_element_type=F32)
    for pg in range(MOBA_BLOCK // PAGE):
        kt_ref[pg] = acc[pg * PAGE:(pg + 1) * PAGE, :].T
    k16_ref[...] = acc.astype(BF16)
    km_ref[...] = jnp.mean(acc, axis=0, keepdims=True)


def _v_proj_kernel(x_ref, nw_ref, sc_ref, sh_ref, w_ref, vt_ref, vh_ref):
    h = _norm_mod(x_ref[...], nw_ref[...], sc_ref[...], sh_ref[...])
    acc = jnp.dot(h.astype(BF16), w_ref[...], preferred_element_type=F32)
    acc_t = acc.T
    for pg in range(MOBA_BLOCK // PAGE):
        vt_ref[pg] = acc_t[:, pg * PAGE:(pg + 1) * PAGE]
    pad = jnp.concatenate([jnp.ones((1, MOBA_BLOCK), F32),
                           jnp.zeros((VROWS - ATT_HEAD_DIM - 1, MOBA_BLOCK), F32)], axis=0)
    for hd in range(ATT_HEADS):
        rows = acc_t[hd * ATT_HEAD_DIM:(hd + 1) * ATT_HEAD_DIM, :]
        vh_ref[hd] = jnp.concatenate([rows, pad], axis=0).astype(BF16)


def _kv_proj(x, nw, scale, shift, w_bf16, which):
    t, k = x.shape
    w = w_bf16.shape[1]
    B = MOBA_BLOCK
    nb = t // B
    ppb = B // PAGE
    one = pl.BlockSpec((1, k), lambda i: (0, 0))
    paged = (jax.ShapeDtypeStruct((t // PAGE, w, PAGE), F32), pl.BlockSpec((ppb, w, PAGE), lambda i: (i, 0, 0)))
    if which == "k":
        body = _k_proj_kernel
        outs = [paged,
                (jax.ShapeDtypeStruct((t, w), BF16), pl.BlockSpec((B, w), lambda i: (i, 0))),
                (jax.ShapeDtypeStruct((nb, 1, w), F32), pl.BlockSpec((None, 1, w), lambda i: (i, 0, 0)))]
    else:
        body = _v_proj_kernel
        outs = [paged,
                (jax.ShapeDtypeStruct((nb, ATT_HEADS, VROWS, B), BF16),
                 pl.BlockSpec((None, ATT_HEADS, VROWS, B), lambda i: (i, 0, 0, 0)))]
    return pl.pallas_call(
        body,
        grid=(nb,),
        in_specs=[pl.BlockSpec((B, k), lambda i: (i, 0)), one, one, one,
                  pl.BlockSpec((k, w), lambda i: (0, 0))],
        out_specs=[o[1] for o in outs],
        out_shape=[o[0] for o in outs],
        compiler_params=_params(("arbitrary",)),
        name=which + "_proj",
    )(x, nw, scale, shift, w_bf16)


def _gated_out_kernel(*refs, has_g, has_final):
    it = iter(refs)
    a_ref = next(it)
    g_ref = next(it) if has_g else None
    x_ref, gate_ref, w_ref = next(it), next(it), next(it)
    fw_ref = next(it) if has_final else None
    o_ref = next(it)
    a = a_ref[...].astype(F32)
    if has_g:
        a = a * _silu(g_ref[...])
    y = x_ref[...] + gate_ref[...] * jnp.dot(a.astype(BF16), w_ref[...], preferred_element_type=F32)
    if has_final:
        ms = jnp.mean(y * y, axis=-1, keepdims=True)
        y = y * lax.rsqrt(ms + EPS) * fw_ref[...]
    o_ref[...] = y


def _gated_out(a, g, x, gate, w_bf16, final_w, *, tm):
    m, k = a.shape
    n = w_bf16.shape[1]
    per_row = gate.shape[0] != 1
    row = lambda width: pl.BlockSpec((tm, width), lambda i: (i, 0))
    one = lambda width: pl.BlockSpec((1, width), lambda i: (0, 0))
    ins, specs = [a], [row(k)]
    if g is not None:
        ins.append(g)
        specs.append(row(k))
    ins += [x, gate, w_bf16]
    specs += [row(n), row(n) if per_row else one(n), pl.BlockSpec((k, n), lambda i: (0, 0))]
    if final_w is not None:
        ins.append(final_w)
        specs.append(one(n))
    return pl.pallas_call(
        functools.partial(_gated_out_kernel, has_g=g is not None, has_final=final_w is not None),
        grid=(m // tm,),
        in_specs=specs,
        out_specs=row(n),
        out_shape=jax.ShapeDtypeStruct((m, n), F32),
        compiler_params=_params(("arbitrary",)),
        name="gated_out",
    )(*ins)


def _ssd_prompt_kernel(x_ref, xbc_ref, z_ref, modv_ref, wdt_ref, wdtT_ref, hrow_ref, hcol_ref,
                       cw_ref, cb_ref, dexp_ref, snw_ref, wout_ref,
                       xo_ref, st_ref, tail_ref,
                       xpad, xg_s, b_s, c_s, xdt_s, abc_s, acst_s, y_s, gn_s):
    L = CHUNK
    ci = pl.program_id(0)
    nci = pl.num_programs(0)

    @pl.when(ci == 0)
    def _():
        xpad[0:8, :] = jnp.zeros((8, CONV_DIM), F32)
        st_ref[...] = jnp.zeros(st_ref.shape, F32)

    x = x_ref[...]
    hn = _norm_mod(x, modv_ref[0:1, :], modv_ref[1:2, :], modv_ref[2:3, :]).astype(BF16)
    dt_raw = jnp.dot(hn, wdt_ref[...], preferred_element_type=F32)
    dt_rawT = lax.dot_general(wdtT_ref[...], hn, NT, preferred_element_type=F32)
    dt = _softplus(dt_raw + hrow_ref[0:1, :])
    dtT = _softplus(dt_rawT + hcol_ref[:, 0:1])
    dA = dt * (-jnp.exp(hrow_ref[1:2, :]))
    dAT = dtT * (-jnp.exp(hcol_ref[:, 1:2]))
    r_i = lax.broadcasted_iota(jnp.int32, (L, L), 0)
    c_i = lax.broadcasted_iota(jnp.int32, (L, L), 1)
    tril = (r_i >= c_i).astype(F32)
    triu = (r_i <= c_i).astype(F32)
    acs = jnp.dot(tril, dA, precision=HI, preferred_element_type=F32)
    acsT = jnp.dot(dAT, triu, precision=HI, preferred_element_type=F32)
    for h in range(SSD_HEADS):
        acst_s[h] = acsT[h:h + 1, :]
        abc_s[h] = jnp.broadcast_to(acsT[h:h + 1, :], (L, 128)).T
    lane = lax.broadcasted_iota(jnp.int32, (L, 128), 1)
    lo_lane = lane < SSD_HEAD_DIM

    def dt_pair(i):
        return jnp.where(lo_lane, jnp.broadcast_to(dt[:, 2 * i:2 * i + 1], (L, 128)),
                         jnp.broadcast_to(dt[:, 2 * i + 1:2 * i + 2], (L, 128)))

    xpad[8:8 + L, :] = xbc_ref[...]
    cw = 512
    for blk in range(CONV_DIM // cw):
        cs = slice(blk * cw, (blk + 1) * cw)
        acc = cb_ref[:, cs] + xpad[5:5 + L, cs] * cw_ref[0:1, cs]
        acc = acc + xpad[6:6 + L, cs] * cw_ref[1:2, cs]
        acc = acc + xpad[7:7 + L, cs] * cw_ref[2:3, cs]
        acc = acc + xpad[8:8 + L, cs] * cw_ref[3:4, cs]
        xc = _silu(acc)
        if blk < 4:
            for q in range(2):
                g = blk * 2 + q
                xg = xc[:, q * 256:(q + 1) * 256]
                xg_s[g] = xg
                xdt_s[2 * g] = xg[:, 0:128] * dt_pair(2 * g)
                xdt_s[2 * g + 1] = xg[:, 128:256] * dt_pair(2 * g + 1)
        elif blk < 6:
            for q in range(4):
                b_s[(blk - 4) * 4 + q] = xc[:, q * 128:(q + 1) * 128].astype(BF16)
        else:
            for q in range(4):
                c_s[(blk - 6) * 4 + q] = xc[:, q * 128:(q + 1) * 128]

    @pl.when(ci == nci - 1)
    def _():
        tail_ref[...] = xpad[5 + L:8 + L, :]

    xpad[5:8, :] = xpad[5 + L:8 + L, :]

    causal = r_i >= c_i
    sub = lax.broadcasted_iota(jnp.int32, (128, 128), 0)
    lo_sub = sub < 64

    def group_compute(g):
        bb = b_s[g]
        cg = c_s[g]
        cbm = lax.dot_general(cg.astype(BF16), bb, NT, preferred_element_type=F32)
        out = []
        for pr in range(2):
            pidx = 2 * g + pr
            xdt_pair = xdt_s[pidx]
            hprev = st_ref[pidx]
            hprev_b = hprev.astype(BF16)
            ypair = jnp.zeros((L, 128), F32)
            dends, cds = [], []
            for hh in range(2):
                h = 2 * pidx + hh
                abc = abc_s[h]
                arow = acst_s[h]
                dec = jnp.exp(jnp.where(causal, abc - arow, NEG))
                mh = (cbm * dec).astype(BF16)
                keep = lo_lane if hh == 0 else jnp.logical_not(lo_lane)
                xm = jnp.where(keep, xdt_pair, 0.0).astype(BF16)
                ypair = ypair + jnp.dot(mh, xm, preferred_element_type=F32)
                cd = (cg * jnp.exp(abc)).astype(BF16)
                yo = lax.dot_general(cd, hprev_b, NT, preferred_element_type=F32)
                ypair = ypair + jnp.where(keep, yo, 0.0)
                tot = abc[L - 1:L, :]
                dends.append(jnp.exp(tot - abc))
                cds.append(jnp.exp(tot))
            dend = jnp.where(lo_lane, dends[0], dends[1])
            xw = (xdt_pair * dend).astype(BF16)
            states = lax.dot_general(xw, bb, TN, preferred_element_type=F32)
            cdm = jnp.where(lo_sub, jnp.broadcast_to(cds[0], (128, 128)),
                            jnp.broadcast_to(cds[1], (128, 128)))
            out.append((pidx, ypair, hprev * cdm + states))
        return out

    def groups_body(i, carry):
        for pidx, ypair, state in group_compute(2 * i) + group_compute(2 * i + 1):
            y_s[pidx] = ypair
            st_ref[pidx] = state
        return carry

    lax.fori_loop(0, SSD_GROUPS // 2, groups_body, 0)

    for g in range(SSD_GROUPS):
        cs = slice(g * 256, (g + 1) * 256)
        yg = jnp.concatenate([y_s[2 * g], y_s[2 * g + 1]], axis=1)
        gg = (yg + xg_s[g] * dexp_ref[:, cs]) * _silu(z_ref[:, cs])
        ms = jnp.mean(gg * gg, axis=-1, keepdims=True)
        gn_s[:, cs] = (gg * lax.rsqrt(ms + EPS) * snw_ref[:, cs]).astype(BF16)

    xo_ref[...] = x + modv_ref[3:4, :] * jnp.dot(gn_s[...], wout_ref[...], preferred_element_type=F32)


def _ssd_prompt(x, proj, modv, wdt, wdtT, hrow, hcol, conv_w, conv_b, dexp, snw, wout):
    t = x.shape[0]
    nc = t // CHUNK
    L = CHUNK
    full = lambda a: pl.BlockSpec(a.shape, lambda c: (0,) * a.ndim)
    return pl.pallas_call(
        _ssd_prompt_kernel,
        grid=(nc,),
        in_specs=[
            pl.BlockSpec((L, D_MODEL), lambda c: (c, 0)),
            pl.BlockSpec((L, CONV_DIM), lambda c: (c, 0)),
            pl.BlockSpec((L, D_INNER), lambda c: (c, 2)),
            full(modv), full(wdt), full(wdtT), full(hrow), full(hcol), full(conv_w), full(conv_b),
            full(dexp), full(snw), full(wout),
        ],
        out_specs=[
            pl.BlockSpec((L, D_MODEL), lambda c: (c, 0)),
            pl.BlockSpec((SSD_HEADS // 2, 128, SSD_STATE), lambda c: (0, 0, 0)),
            pl.BlockSpec((SSD_CONV - 1, CONV_DIM), lambda c: (0, 0)),
        ],
        out_shape=[
            jax.ShapeDtypeStruct((t, D_MODEL), F32),
            jax.ShapeDtypeStruct((SSD_HEADS // 2, 128, SSD_STATE), F32),
            jax.ShapeDtypeStruct((SSD_CONV - 1, CONV_DIM), F32),
        ],
        scratch_shapes=[
            pltpu.VMEM((8 + L, CONV_DIM), F32),
            pltpu.VMEM((SSD_GROUPS, L, 256), F32),
            pltpu.VMEM((SSD_GROUPS, L, SSD_STATE), BF16),
            pltpu.VMEM((SSD_GROUPS, L, SSD_STATE), F32),
            pltpu.VMEM((SSD_HEADS // 2, L, 128), F32),
            pltpu.VMEM((SSD_HEADS, L, 128), F32),
            pltpu.VMEM((SSD_HEADS, 1, L), F32),
            pltpu.VMEM((SSD_HEADS // 2, L, 128), F32),
            pltpu.VMEM((L, D_INNER), BF16),
        ],
        compiler_params=_params(("arbitrary",)),
        name="ssd_prompt",
    )(x, proj, proj, modv, wdt, wdtT, hrow, hcol, conv_w, conv_b, dexp, snw, wout)


def _expansion(heads, width):
    e = np.zeros((heads, heads * width), np.float32)
    for h in range(heads):
        e[h, h * width:(h + 1) * width] = 1.0
    return jnp.asarray(e, BF16)


def _split_mod(mod):
    return mod[..., :D_MODEL], mod[..., D_MODEL:2 * D_MODEL], mod[..., 2 * D_MODEL:]


def _prompt_ssd_layer(x, mod, norm_w, w_in, conv_w, conv_b, dt_bias, a_log, d_skip, ssd_norm_w, w_out):
    shift, scale, gate = _split_mod(mod)
    row = lambda v: v.reshape(1, -1)
    w_main = jnp.concatenate([w_in[:, D_INNER:D_INNER + CONV_DIM], w_in[:, :D_INNER]], axis=1).astype(BF16)
    w_dt = w_in[:, D_INNER + CONV_DIM:].astype(BF16)
    proj = _norm_linear(x, row(norm_w), row(scale), row(shift), w_main, tm=512, tn=2048)
    zeros = jnp.zeros((4, D_MODEL), F32)
    modv = jnp.concatenate([row(norm_w), row(scale), row(shift), row(gate), zeros], axis=0)
    hrow = jnp.concatenate([row(dt_bias), row(a_log), jnp.zeros((6, SSD_HEADS), F32)], axis=0)
    hcol = hrow.T
    dexp = jnp.repeat(d_skip, SSD_HEAD_DIM).reshape(1, D_INNER)
    return _ssd_prompt(x, proj, modv, w_dt, w_dt.T, hrow, hcol, conv_w, row(conv_b), dexp,
                       row(ssd_norm_w), w_out.astype(BF16))


def _topk_block_mask(gate, blk, n_valid):
    nb = gate.shape[0]
    ninf = jnp.float32(-jnp.inf)
    blk_f = blk.astype(F32)
    g = jnp.where(blk < n_valid, gate, ninf)
    sel = jnp.zeros(gate.shape, jnp.bool_)
    for _ in range(MOBA_TOPK):
        m = jnp.max(g, axis=0, keepdims=True)
        first = jnp.min(jnp.where(g == m, blk_f, float(nb)), axis=0, keepdims=True)
        pick = jnp.logical_and(blk_f == first, m > ninf)
        sel = jnp.logical_or(sel, pick)
        g = jnp.where(pick, ninf, g)
    return jnp.where(sel, 0.0, NEG)


def _moba_select_kernel(q_ref, km_ref, o_ref):
    c = pl.program_id(0)
    nb = km_ref.shape[0]
    tq = q_ref.shape[0]
    lane = lax.broadcasted_iota(jnp.int32, (tq, 128), 1)
    blk = lax.broadcasted_iota(jnp.int32, (nb, tq), 0)
    for p in range(ATT_HEADS // 2):
        qp = q_ref[:, p * 128:(p + 1) * 128]
        kmp = km_ref[:, p * 128:(p + 1) * 128]
        for hh in range(2):
            keep = (lane < 64) if hh == 0 else (lane >= 64)
            qm = jnp.where(keep, qp, 0.0)
            gate = _dot3_nt(kmp, qm)
            o_ref[2 * p + hh] = _topk_block_mask(gate, blk, c)


def _moba_select(q, kmean):
    t, w = q.shape
    nb = kmean.shape[0]
    return pl.pallas_call(
        _moba_select_kernel,
        grid=(nb,),
        in_specs=[pl.BlockSpec((MOBA_BLOCK, w), lambda c: (c, 0)),
                  pl.BlockSpec((nb, w), lambda c: (0, 0))],
        out_specs=pl.BlockSpec((ATT_HEADS, nb, MOBA_BLOCK), lambda c: (0, 0, c)),
        out_shape=jax.ShapeDtypeStruct((ATT_HEADS, nb, t), F32),
        compiler_params=_params(("arbitrary",)),
        name="moba_select",
    )(q, kmean)


LOG2E = 1.4426950408889634


def _moba_attn_kernel(slopes_ref, q_ref, k_ref, vh_ref, mask_ref, o_ref, spa_s, spb_s):
    B = MOBA_BLOCK
    hd = ATT_HEAD_DIM
    p = pl.program_id(0)
    c = pl.program_id(1)
    q = q_ref[...] * (hd ** -0.5 * LOG2E)
    lane = lax.broadcasted_iota(jnp.int32, (B, 128), 1)
    qh = [jnp.where(lane < hd, q, 0.0).astype(BF16), jnp.where(lane >= hd, q, 0.0).astype(BF16)]
    s_i = lax.broadcasted_iota(jnp.int32, (B, B), 0)
    t_i = lax.broadcasted_iota(jnp.int32, (B, B), 1)
    dloc = (s_i - t_i).astype(F32)
    sl = [slopes_ref[2 * p] * LOG2E, slopes_ref[2 * p + 1] * LOG2E]
    bh = [dloc * sl[0], dloc * sl[1]]

    def scores(j, hh):
        return lax.dot_general(k_ref[j], qh[hh], NT, preferred_element_type=F32) + bh[hh]

    sps = [jnp.where(s_i <= t_i, scores(c, hh), NEG) for hh in range(2)]
    ms = [jnp.max(sp, axis=0, keepdims=True) for sp in sps]
    pes = [jnp.exp2(sp - m).astype(BF16) for sp, m in zip(sps, ms)]
    accs = [jnp.dot(vh_ref[c, hh], pes[hh], preferred_element_type=F32) for hh in range(2)]

    combos = [(hh, n) for hh in range(2) for n in range(2)]

    def trip_blocks(t):
        return [jnp.minimum(2 * t + n, c) for n in range(2)]

    def put_scores(t, buf):
        js = trip_blocks(t)
        for k, (hh, n) in enumerate(combos):
            buf[k] = scores(js[n], hh)

    def softmax_pv(t, buf, carry):
        m_old, acc_old = carry[:2], carry[2:]
        js = trip_blocks(t)
        radd = {(hh, n): mask_ref[hh, pl.ds(js[n], 1), :] - (c - js[n]).astype(F32) * float(B) * sl[hh]
                for hh, n in combos}
        mx = {cm: jnp.max(buf[k], axis=0, keepdims=True) + radd[cm] for k, cm in enumerate(combos)}
        m_new = [jnp.maximum(m_old[hh], jnp.maximum(mx[(hh, 0)], mx[(hh, 1)])) for hh in range(2)]
        pe = {(hh, n): jnp.exp2(buf[k] - (m_new[hh] - radd[(hh, n)])).astype(BF16)
              for k, (hh, n) in enumerate(combos)}
        pv = {(hh, n): jnp.dot(vh_ref[js[n], hh], pe[(hh, n)], preferred_element_type=F32) for hh, n in combos}
        acc_new = [acc_old[hh] * jnp.exp2(m_old[hh] - m_new[hh]) + (pv[(hh, 0)] + pv[(hh, 1)])
                   for hh in range(2)]
        return tuple(m_new) + tuple(acc_new)

    put_scores(0, spa_s)

    def two_trips(t0, carry):
        put_scores(t0 + 1, spb_s)
        carry = softmax_pv(t0, spa_s, carry)
        put_scores(t0 + 2, spa_s)
        return softmax_pv(t0 + 1, spb_s, carry)

    n_bulk = lax.shift_right_logical(c, 3)
    n_rest = lax.shift_right_logical(c - 8 * n_bulk + 3, 2)
    res = lax.fori_loop(0, n_bulk, lambda i, carry: two_trips(4 * i + 2, two_trips(4 * i, carry)),
                        tuple(ms) + tuple(accs))
    res = lax.fori_loop(0, n_rest, lambda i, carry: two_trips(4 * n_bulk + 2 * i, carry), res)
    outs = [acc[0:hd, :] / acc[hd:hd + 1, :] for acc in res[2:4]]
    o_ref[...] = jnp.concatenate(outs, axis=0).T


def _moba_attn(q, k_blocks, vh_blocks, mask, slopes):
    t, w = q.shape
    nb = k_blocks.shape[0]
    B = MOBA_BLOCK
    return pl.pallas_call(
        _moba_attn_kernel,
        grid=(ATT_HEADS // 2, nb),
        in_specs=[
            pl.BlockSpec(memory_space=pltpu.SMEM),
            pl.BlockSpec((B, 128), lambda p, c: (c, p)),
            pl.BlockSpec((nb, B, 128), lambda p, c: (0, 0, p)),
            pl.BlockSpec((nb, 2, VROWS, B), lambda p, c: (0, p, 0, 0)),
            pl.BlockSpec((2, nb, B), lambda p, c: (p, 0, c)),
        ],
        out_specs=pl.BlockSpec((B, 128), lambda p, c: (c, p)),
        out_shape=jax.ShapeDtypeStruct((t, w), F32),
        scratch_shapes=[pltpu.VMEM((4, B, B), F32), pltpu.VMEM((4, B, B), F32)],
        compiler_params=_params(("arbitrary", "arbitrary")),
        name="moba_attn",
    )(slopes, q, k_blocks, vh_blocks, mask)


def _alibi_slopes():
    return jnp.asarray(2.0 ** (-8.0 * np.arange(1, ATT_HEADS + 1) / ATT_HEADS), dtype=F32)


def _prompt_moba_layer(x, mod, norm_w, w_in, w_out, final_w):
    shift, scale, gate = _split_mod(mod)
    row = lambda v: v.reshape(1, -1)
    nw, sc, sh = row(norm_w), row(scale), row(shift)
    wq, wk, wv, wg = (w_in[:, i * ATT_WIDTH:(i + 1) * ATT_WIDTH].astype(BF16) for i in range(4))
    B = MOBA_BLOCK
    nb = x.shape[0] // B
    q = _norm_linear(x, nw, sc, sh, wq, tm=512, tn=1024)
    g = _norm_linear(x, nw, sc, sh, wg, tm=512, tn=1024)
    kt, k16, kmean = _kv_proj(x, nw, sc, sh, wk, "k")
    vt, vh16 = _kv_proj(x, nw, sc, sh, wv, "v")
    mask = _moba_select(q, kmean.reshape(nb, ATT_WIDTH))
    o = _moba_attn(q, k16.reshape(nb, B, ATT_WIDTH), vh16, mask, _alibi_slopes())
    y = _gated_out(o, g, x, row(gate), w_out.astype(BF16), row(final_w), tm=512)
    return y, kt, vt


def _ssd_step_kernel(xbc_ref, z_ref, dtr_ref, cst_ref, st_ref, hrow_ref, cw_ref, cb_ref, dexp_ref, snw_ref,
                     e64_ref, sel_ref, gn_ref, sto_ref, cso_ref):
    xbc = xbc_ref[...]
    conv = cb_ref[...] + cst_ref[0:1, :] * cw_ref[0:1, :]
    conv = conv + cst_ref[1:2, :] * cw_ref[1:2, :]
    conv = conv + cst_ref[2:3, :] * cw_ref[2:3, :]
    conv = conv + xbc * cw_ref[3:4, :]
    cso_ref[0:2, :] = cst_ref[1:3, :]
    cso_ref[2:3, :] = xbc
    xc = _silu(conv)
    xs = xc[:, :D_INNER]
    dt = _softplus(dtr_ref[:, :SSD_HEADS] + hrow_ref[0:1, :])
    dec = jnp.exp(dt * (-jnp.exp(hrow_ref[1:2, :])))
    both = _expand_lanes(jnp.concatenate([dt, dec, jnp.zeros((6, SSD_HEADS), F32)], axis=0), e64_ref)
    xdt = xs * both[0:1, :]
    dec_exp = both[1:2, :]
    pieces = [p.astype(F32) for p in _split3(xdt) + _split3(dec_exp)]
    stack = jnp.concatenate(pieces + [jnp.zeros((10, D_INNER), F32)], axis=0).astype(BF16)
    cols = lax.dot_general(stack, sel_ref[...], TN, preferred_element_type=F32)
    ys = []
    for i in range(SSD_HEADS // 2):
        g = i // 2
        brow = xc[:, D_INNER + g * 128:D_INNER + (g + 1) * 128]
        crow = xc[:, D_INNER + GN + g * 128:D_INNER + GN + (g + 1) * 128]
        xcol = cols[i * 128:(i + 1) * 128, 0:1]
        dcol = cols[i * 128:(i + 1) * 128, 1:2]
        hn = st_ref[i] * dcol + xcol * brow
        sto_ref[i] = hn
        c8 = jnp.broadcast_to(crow, (8, SSD_STATE)).astype(BF16)
        ys.append(lax.dot_general(c8, hn.astype(BF16), NT, preferred_element_type=F32)[0:1, :])
    y = jnp.concatenate(ys, axis=1)
    gg = (y + xs * dexp_ref[...]) * _silu(z_ref[...])
    outs = []
    for g in range(SSD_GROUPS):
        gs = gg[:, g * 256:(g + 1) * 256]
        ms = jnp.mean(gs * gs, axis=-1, keepdims=True)
        outs.append(gs * lax.rsqrt(ms + EPS))
    gn_ref[...] = jnp.concatenate(outs, axis=1) * snw_ref[...]


def _ssd_step(proj, dt_raw, conv_state, ssm_state, hrow, conv_w, conv_b, dexp, snw, e64):
    db = proj.shape[0]
    sel = np.zeros((16, 128), np.float32)
    sel[0:3, 0] = 1.0
    sel[3:6, 1] = 1.0
    sel = jnp.asarray(sel, BF16)
    full = lambda a: pl.BlockSpec(a.shape, lambda b: (0,) * a.ndim)
    hp = SSD_HEADS // 2
    return pl.pallas_call(
        _ssd_step_kernel,
        grid=(db,),
        in_specs=[
            pl.BlockSpec((None, 1, CONV_DIM), lambda b: (b, 0, 0)),
            pl.BlockSpec((None, 1, D_INNER), lambda b: (b, 0, 2)),
            pl.BlockSpec((None, 1, 128), lambda b: (b, 0, 0)),
            pl.BlockSpec((None, SSD_CONV - 1, CONV_DIM), lambda b: (b, 0, 0)),
            pl.BlockSpec((None, hp, 128, SSD_STATE), lambda b: (b, 0, 0, 0)),
            full(hrow), full(conv_w), full(conv_b), full(dexp), full(snw), full(e64), full(sel),
        ],
        out_specs=[
            pl.BlockSpec((None, 1, D_INNER), lambda b: (b, 0, 0)),
            pl.BlockSpec((None, hp, 128, SSD_STATE), lambda b: (b, 0, 0, 0)),
            pl.BlockSpec((None, SSD_CONV - 1, CONV_DIM), lambda b: (b, 0, 0)),
        ],
        out_shape=[
            jax.ShapeDtypeStruct((db, 1, D_INNER), F32),
            jax.ShapeDtypeStruct((db, hp, 128, SSD_STATE), F32),
            jax.ShapeDtypeStruct((db, SSD_CONV - 1, CONV_DIM), F32),
        ],
        compiler_params=_params(("arbitrary",)),
        name="ssd_step",
    )(proj, proj, dt_raw, conv_state, ssm_state, hrow, conv_w, conv_b, dexp, snw, e64, sel)


def _sample_ssd_layer(x, mod, norm_w, w_in, conv_w, conv_b, dt_bias, a_log, d_skip, ssd_norm_w, w_out,
                      conv_state, ssm_state):
    db = x.shape[0]
    shift, scale, gate = _split_mod(mod)
    row = lambda v: v.reshape(1, -1)
    w_main = jnp.concatenate([w_in[:, D_INNER:D_INNER + CONV_DIM], w_in[:, :D_INNER]], axis=1).astype(BF16)
    w_dt = jnp.pad(w_in[:, D_INNER + CONV_DIM:], ((0, 0), (0, 128 - SSD_HEADS))).astype(BF16)
    proj = _norm_linear(x, row(norm_w), scale, shift, w_main, tm=db, tn=2048)
    dt_raw = _norm_linear(x, row(norm_w), scale, shift, w_dt, tm=db, tn=128)
    hrow = jnp.concatenate([row(dt_bias), row(a_log), jnp.zeros((6, SSD_HEADS), F32)], axis=0)
    dexp = jnp.repeat(d_skip, SSD_HEAD_DIM).reshape(1, D_INNER)
    gn, st, cs = _ssd_step(proj.reshape(db, 1, -1), dt_raw.reshape(db, 1, 128), conv_state,
                           ssm_state.reshape(db, SSD_HEADS // 2, 128, SSD_STATE), hrow, conv_w, row(conv_b),
                           dexp, row(ssd_norm_w), _expansion(SSD_HEADS, SSD_HEAD_DIM))
    x1 = _gated_out(gn.reshape(db, D_INNER), None, x, gate, w_out.astype(BF16), None, tm=db)
    return x1, st, cs


PAGES_PER_STEP = 16
PAGES_PER_BLOCK = MOBA_BLOCK // PAGE


def _paged_mean_kernel(pt_ref, *refs):
    pages, o_ref = refs[:PAGES_PER_STEP], refs[PAGES_PER_STEP]
    s = pl.program_id(1)
    w, nb = o_ref.shape
    bps = PAGES_PER_STEP // PAGES_PER_BLOCK

    @pl.when(s == 0)
    def _():
        o_ref[...] = jnp.zeros((w, nb), F32)

    lane = lax.broadcasted_iota(jnp.int32, (w, nb), 1)
    cur = o_ref[...]
    for r in range(bps):
        tot = pages[2 * r][...].reshape(w, PAGE) + pages[2 * r + 1][...].reshape(w, PAGE)
        col = jnp.sum(tot, axis=-1, keepdims=True) * (1.0 / MOBA_BLOCK)
        cur = jnp.where(lane == s * bps + r, col, cur)
    o_ref[...] = cur


def _paged_mean(cache_t, page_table):
    db, n_pages = page_table.shape
    nsteps = n_pages // PAGES_PER_STEP
    nblk = n_pages // PAGES_PER_BLOCK

    def page_spec(r):
        return pl.BlockSpec((None, ATT_HEADS, ATT_HEAD_DIM, PAGE),
                            lambda b, s, pt: (pt[b, s * PAGES_PER_STEP + r], 0, 0, 0))

    return pl.pallas_call(
        _paged_mean_kernel,
        grid_spec=pltpu.PrefetchScalarGridSpec(
            num_scalar_prefetch=1,
            grid=(db, nsteps),
            in_specs=[page_spec(r) for r in range(PAGES_PER_STEP)],
            out_specs=pl.BlockSpec((None, ATT_WIDTH, nblk), lambda b, s, pt: (b, 0, 0)),
        ),
        out_shape=jax.ShapeDtypeStruct((db, ATT_WIDTH, nblk), F32),
        compiler_params=_params(("arbitrary", "arbitrary")),
        name="paged_mean",
    )(page_table, *([cache_t] * PAGES_PER_STEP))


def _sample_select_kernel(q_ref, km_ref, o_ref):
    w, nb = km_ref.shape
    prod = km_ref[...] * q_ref[...]
    gate = jnp.sum(prod.reshape(ATT_HEADS, ATT_HEAD_DIM, nb), axis=1)
    blk_f = lax.broadcasted_iota(jnp.int32, gate.shape, 1).astype(F32)
    lane = lax.broadcasted_iota(jnp.int32, (ATT_HEADS, 128), 1)
    ninf = jnp.float32(-jnp.inf)
    g = gate
    out = jnp.zeros((ATT_HEADS, 128), F32)
    for r in range(MOBA_TOPK):
        m = jnp.max(g, axis=-1, keepdims=True)
        first = jnp.min(jnp.where(g == m, blk_f, float(nb)), axis=-1, keepdims=True)
        out = jnp.where(lane == r, first, out)
        g = jnp.where(blk_f == first, ninf, g)
    o_ref[...] = out.astype(jnp.int32)


def _sample_select(q_col, kmean_t):
    db, w, nb = kmean_t.shape
    return pl.pallas_call(
        _sample_select_kernel,
        grid=(db,),
        in_specs=[pl.BlockSpec((None, w, 1), lambda b: (b, 0, 0)),
                  pl.BlockSpec((None, w, nb), lambda b: (b, 0, 0))],
        out_specs=pl.BlockSpec((None, ATT_HEADS, 128), lambda b: (b, 0, 0)),
        out_shape=jax.ShapeDtypeStruct((db, ATT_HEADS, 128), jnp.int32),
        compiler_params=_params(("arbitrary",)),
        name="sample_select",
    )(q_col, kmean_t)


def _sel_copies(pt_ref, idx_ref, cache_hbm, buf, sem, b):
    out = []
    for h in range(ATT_HEADS):
        for r in range(MOBA_TOPK):
            blk = idx_ref[b, h * MOBA_TOPK + r]
            for half in range(PAGES_PER_BLOCK):
                page = pt_ref[b, blk * PAGES_PER_BLOCK + half]
                out.append(pltpu.make_async_copy(cache_hbm.at[page, h],
                                                 buf.at[h, r * PAGES_PER_BLOCK + half], sem))
    return out


def _sample_attn_kernel(pt_ref, idx_ref, q_ref, kn_ref, vn_ref, idxv_ref, slopes_ref, ck_hbm, cv_hbm, o_ref,
                        kbuf, vbuf, sems, *, past_len):
    b = pl.program_id(0)
    kcopies = _sel_copies(pt_ref, idx_ref, ck_hbm, kbuf, sems.at[0], b)
    vcopies = _sel_copies(pt_ref, idx_ref, cv_hbm, vbuf, sems.at[1], b)
    for cp in kcopies + vcopies:
        cp.start()
    for cp in kcopies + vcopies:
        cp.wait()
    nsel = MOBA_TOPK * MOBA_BLOCK
    npg = MOBA_TOPK * PAGES_PER_BLOCK
    scale = ATT_HEAD_DIM ** -0.5
    q = q_ref[...]
    rows = []
    for h in range(ATT_HEADS):
        q8 = jnp.broadcast_to(q[h:h + 1, :], (8, ATT_HEAD_DIM)).astype(BF16)
        kcat = jnp.concatenate([kbuf[h, i] for i in range(npg)], axis=1).astype(BF16)
        rows.append(jnp.dot(q8, kcat, preferred_element_type=F32)[0:1, :])
    s = jnp.concatenate(rows, axis=0) * scale
    local = lax.broadcasted_iota(jnp.int32, (ATT_HEADS, nsel), 1)
    within = jnp.bitwise_and(local, MOBA_BLOCK - 1).astype(F32)
    which = jnp.right_shift(local, MOBA_BLOCK.bit_length() - 1)
    starts = (idxv_ref[...] * MOBA_BLOCK).astype(F32)
    start = jnp.zeros((ATT_HEADS, nsel), F32)
    for r in range(MOBA_TOPK):
        start = jnp.where(which == r, starts[:, r:r + 1], start)
    slope = slopes_ref[:, 0:1]
    s = s - slope * (float(past_len) - (start + within))
    s_self = jnp.sum(q.astype(BF16).astype(F32) * kn_ref[...].astype(BF16).astype(F32),
                     axis=-1, keepdims=True) * scale
    m = jnp.maximum(jnp.max(s, axis=-1, keepdims=True), s_self)
    pe = jnp.exp(s - m)
    p_self = jnp.exp(s_self - m)
    l = jnp.sum(pe, axis=-1, keepdims=True) + p_self
    pe16 = pe.astype(BF16)
    rows = []
    for h in range(ATT_HEADS):
        vcat = jnp.concatenate([vbuf[h, i] for i in range(npg)], axis=1).astype(BF16)
        p8 = jnp.broadcast_to(pe16[h:h + 1, :], (8, nsel))
        rows.append(lax.dot_general(p8, vcat, NT, preferred_element_type=F32)[0:1, :])
    o_ref[...] = (jnp.concatenate(rows, axis=0) + p_self * vn_ref[...]) / l


def _sample_attn(q, k_new, v_new, cache_kt, cache_vt, page_table, idx, idx_lanes, slopes):
    db, n_pages = page_table.shape
    npg = MOBA_TOPK * PAGES_PER_BLOCK
    vec = pl.BlockSpec((None, ATT_HEADS, ATT_HEAD_DIM), lambda b, pt, ix: (b, 0, 0))
    slopes = jnp.broadcast_to(slopes[:, None], (ATT_HEADS, 128))
    return pl.pallas_call(
        functools.partial(_sample_attn_kernel, past_len=n_pages * PAGE),
        grid_spec=pltpu.PrefetchScalarGridSpec(
            num_scalar_prefetch=2,
            grid=(db,),
            in_specs=[vec, vec, vec,
                      pl.BlockSpec((None, ATT_HEADS, 128), lambda b, pt, ix: (b, 0, 0)),
                      pl.BlockSpec((ATT_HEADS, 128), lambda b, pt, ix: (0, 0)),
                      pl.BlockSpec(memory_space=pl.ANY), pl.BlockSpec(memory_space=pl.ANY)],
            out_specs=vec,
            scratch_shapes=[pltpu.VMEM((ATT_HEADS, npg, ATT_HEAD_DIM, PAGE), F32),
                            pltpu.VMEM((ATT_HEADS, npg, ATT_HEAD_DIM, PAGE), F32),
                            pltpu.SemaphoreType.DMA((2,))],
        ),
        out_shape=jax.ShapeDtypeStruct((db, ATT_HEADS, ATT_HEAD_DIM), F32),
        compiler_params=_params(("arbitrary",)),
        name="sample_attn",
    )(page_table, idx, q, k_new, v_new, idx_lanes, slopes, cache_kt, cache_vt)


def _sample_moba_layer(x, mod, norm_w, w_in, w_out, final_w, cache_kt, cache_vt, page_table):
    db = x.shape[0]
    shift, scale, gate = _split_mod(mod)
    row = lambda v: v.reshape(1, -1)
    proj = _norm_linear(x, row(norm_w), scale, shift, w_in.astype(BF16), tm=db, tn=1024)
    hs = (db, ATT_HEADS, ATT_HEAD_DIM)
    q, k, v = (proj[:, i * ATT_WIDTH:(i + 1) * ATT_WIDTH].reshape(hs) for i in range(3))
    g = proj[:, 3 * ATT_WIDTH:]
    idx_lanes = _sample_select(q.reshape(db, ATT_WIDTH, 1), _paged_mean(cache_kt, page_table))
    idx = idx_lanes[:, :, :MOBA_TOPK].reshape(db, ATT_HEADS * MOBA_TOPK)
    o = _sample_attn(q, k, v, cache_kt, cache_vt, page_table, idx, idx_lanes, _alibi_slopes())
    y = _gated_out(o.reshape(db, ATT_WIDTH), g, x, gate, w_out.astype(BF16), row(final_w), tm=db)
    return y, k, v


def kernel(x_prompt, x_sample, cache_k, cache_v, state_ssm, state_conv, page_table, c_prompt, c_sample,
           ada_w, ada_b, norm_w, ssd_w_in, ssd_conv_w, ssd_conv_b, ssd_dt_bias, ssd_a_log, ssd_d,
           ssd_norm_w, ssd_w_out, att_w_in, att_w_out, final_norm_w):
    bsz, t, d = x_prompt.shape
    db = x_sample.shape[0]
    c_rows = jnp.concatenate([c_prompt, c_sample, jnp.zeros((7, d), F32)], axis=0)
    mod = _ada_mod(c_rows, ada_w, ada_b)

    xp = x_prompt[0]
    xp, ssm_p, conv_p = _prompt_ssd_layer(xp, mod[0, 0], norm_w[0], ssd_w_in[0], ssd_conv_w[0], ssd_conv_b[0],
                                          ssd_dt_bias[0], ssd_a_log[0], ssd_d[0], ssd_norm_w[0], ssd_w_out[0])
    y_p, k_p, v_p = _prompt_moba_layer(xp, mod[1, 0], norm_w[1], att_w_in[0], att_w_out[0], final_norm_w)

    xs = x_sample[:, 0]
    xs, ssm_s, conv_s = _sample_ssd_layer(xs, mod[0, 1:1 + db], norm_w[0], ssd_w_in[0], ssd_conv_w[0], ssd_conv_b[0],
                                          ssd_dt_bias[0], ssd_a_log[0], ssd_d[0], ssd_norm_w[0], ssd_w_out[0],
                                          state_conv[0], state_ssm[0])
    y_s, k_s, v_s = _sample_moba_layer(xs, mod[1, 1:1 + db], norm_w[1], att_w_in[0], att_w_out[0], final_norm_w,
                                       jnp.transpose(cache_k[0], (0, 2, 3, 1)),
                                       jnp.transpose(cache_v[0], (0, 2, 3, 1)), page_table)

    n_pg = t // PAGE
    kv_s_shape = (1, db, 1, ATT_HEADS, ATT_HEAD_DIM)

    def paged_kv(a):
        a = a.reshape(n_pg, ATT_HEADS, ATT_HEAD_DIM, PAGE)
        return jnp.transpose(a, (0, 3, 1, 2)).reshape(1, bsz, n_pg, PAGE, ATT_HEADS, ATT_HEAD_DIM)

    return (y_p.reshape(bsz, t, d),
            y_s.reshape(db, 1, d),
            paged_kv(k_p), paged_kv(v_p),
            k_s.reshape(kv_s_shape), v_s.reshape(kv_s_shape),
            ssm_p.reshape(1, bsz, SSD_HEADS, SSD_HEAD_DIM, SSD_STATE),
            conv_p.reshape(1, bsz, SSD_CONV - 1, CONV_DIM),
            ssm_s.reshape(1, db, SSD_HEADS, SSD_HEAD_DIM, SSD_STATE),
            conv_s.reshape(1, db, SSD_CONV - 1, CONV_DIM))
```

```python
import functools

import numpy as np
import jax
import jax.numpy as jnp
from jax import lax
from jax.experimental import pallas as pl
from jax.experimental.pallas import tpu as pltpu

F32 = jnp.float32
BF16 = jnp.bfloat16
HI = lax.Precision.HIGHEST

D_MODEL = 1024
D_INNER = 2048
SSD_HEADS = 32
SSD_HEAD_DIM = 64
SSD_GROUPS = 8
SSD_STATE = 128
SSD_CONV = 4
CHUNK = 128
GN = SSD_GROUPS * SSD_STATE
CONV_DIM = D_INNER + 2 * GN
ATT_HEADS = 16
ATT_HEAD_DIM = 64
ATT_WIDTH = ATT_HEADS * ATT_HEAD_DIM
MOBA_BLOCK = 256
MOBA_TOPK = 3
PAGE = 128
EPS = 1e-6
NEG = -1e30

VMEM_LIMIT = 56 * 1024 * 1024

NT = (((1,), (1,)), ((), ()))
TN = (((0,), (0,)), ((), ()))


def _silu(x):
    return x / (1.0 + jnp.exp(-x))


def _softplus(x):
    return jnp.maximum(x, 0.0) + jnp.log1p(jnp.exp(-jnp.abs(x)))


def _split3(x):
    hi = x.astype(BF16)
    r1 = x - hi.astype(F32)
    mid = r1.astype(BF16)
    lo = (r1 - mid.astype(F32)).astype(BF16)
    return hi, mid, lo


def _expand_lanes(x, e_ref):
    hi, mid, lo = _split3(x)
    e = e_ref[...]
    out = jnp.dot(hi, e, preferred_element_type=F32)
    out = out + jnp.dot(mid, e, preferred_element_type=F32)
    return out + jnp.dot(lo, e, preferred_element_type=F32)


def _dot3_nt(a, b):
    ah, am, _ = _split3(a)
    bh, bm, _ = _split3(b)
    dot = lambda x, y: lax.dot_general(x, y, NT, preferred_element_type=F32)
    return dot(ah, bh) + (dot(ah, bm) + dot(am, bh))


def _params(sem):
    return pltpu.CompilerParams(dimension_semantics=sem, vmem_limit_bytes=VMEM_LIMIT)


def _ada_kernel(c_ref, w_ref, b_ref, o_ref):
    sc = _silu(c_ref[...])
    o_ref[...] = jnp.dot(sc, w_ref[...], precision=HI, preferred_element_type=F32) + b_ref[...]


def _ada_mod(c_rows, ada_w, ada_b):
    depth, d, n3 = ada_w.shape
    m = c_rows.shape[0]
    tn = 1024
    return pl.pallas_call(
        _ada_kernel,
        grid=(depth, n3 // tn),
        in_specs=[
            pl.BlockSpec((m, d), lambda i, j: (0, 0)),
            pl.BlockSpec((None, d, tn), lambda i, j: (i, 0, j)),
            pl.BlockSpec((None, 1, tn), lambda i, j: (i, 0, j)),
        ],
        out_specs=pl.BlockSpec((None, m, tn), lambda i, j: (i, 0, j)),
        out_shape=jax.ShapeDtypeStruct((depth, m, n3), F32),
        compiler_params=_params(("arbitrary", "arbitrary")),
        name="ada_mod",
    )(c_rows, ada_w, ada_b.reshape(depth, 1, n3))


def _norm_mod(x, nw, scale, shift):
    ms = jnp.mean(x * x, axis=-1, keepdims=True)
    h = x * lax.rsqrt(ms + EPS) * nw
    return h * (1.0 + scale) + shift


def _norm_linear_kernel(x_ref, nw_ref, sc_ref, sh_ref, w_ref, o_ref):
    h = _norm_mod(x_ref[...], nw_ref[...], sc_ref[...], sh_ref[...])
    o_ref[...] = jnp.dot(h.astype(BF16), w_ref[...], preferred_element_type=F32)


def _norm_linear(x, nw, scale, shift, w_bf16, *, tm, tn):
    m, k = x.shape
    n = w_bf16.shape[1]
    per_row = scale.shape[0] != 1
    vec = (pl.BlockSpec((tm, k), lambda j, i: (i, 0)) if per_row
           else pl.BlockSpec((1, k), lambda j, i: (0, 0)))
    return pl.pallas_call(
        _norm_linear_kernel,
        grid=(n // tn, m // tm),
        in_specs=[
            pl.BlockSpec((tm, k), lambda j, i: (i, 0)),
            pl.BlockSpec((1, k), lambda j, i: (0, 0)),
            vec, vec,
            pl.BlockSpec((k, tn), lambda j, i: (0, j)),
        ],
        out_specs=pl.BlockSpec((tm, tn), lambda j, i: (i, j)),
        out_shape=jax.ShapeDtypeStruct((m, n), F32),
        compiler_params=_params(("arbitrary", "arbitrary")),
        name="norm_linear",
    )(x, nw, scale, shift, w_bf16)


VROWS = 80


def _k_proj_kernel(x_ref, nw_ref, sc_ref, sh_ref, w_ref, kt_ref, k16_ref, km_ref):
    h = _norm_mod(x_ref[...], nw_ref[...], sc_ref[...], sh_ref[...])
    acc = jnp.dot(h.astype(BF16), w_ref[...], preferred_element_type=F32)
    for pg in range(MOBA_BLOCK // PAGE):
        kt_ref[pg] = acc[pg * PAGE:(pg + 1) * PAGE, :].T
    k16_ref[...] = acc.astype(BF16)
    km_ref[...] = jnp.mean(acc, axis=0, keepdims=True)


def _v_proj_kernel(x_ref, nw_ref, sc_ref, sh_ref, w_ref, vt_ref, vh_ref):
    h = _norm_mod(x_ref[...], nw_ref[...], sc_ref[...], sh_ref[...])
    acc = jnp.dot(h.astype(BF16), w_ref[...], preferred_element_type=F32)
    acc_t = acc.T
    for pg in range(MOBA_BLOCK // PAGE):
        vt_ref[pg] = acc_t[:, pg * PAGE:(pg + 1) * PAGE]
    pad = jnp.concatenate([jnp.ones((1, MOBA_BLOCK), F32),
                           jnp.zeros((VROWS - ATT_HEAD_DIM - 1, MOBA_BLOCK), F32)], axis=0)
    for hd in range(ATT_HEADS):
        rows = acc_t[hd * ATT_HEAD_DIM:(hd + 1) * ATT_HEAD_DIM, :]
        vh_ref[hd] = jnp.concatenate([rows, pad], axis=0).astype(BF16)


def _kv_proj(x, nw, scale, shift, w_bf16, which):
    t, k = x.shape
    w = w_bf16.shape[1]
    B = MOBA_BLOCK
    nb = t // B
    ppb = B // PAGE
    one = pl.BlockSpec((1, k), lambda i: (0, 0))
    paged = (jax.ShapeDtypeStruct((t // PAGE, w, PAGE), F32), pl.BlockSpec((ppb, w, PAGE), lambda i: (i, 0, 0)))
    if which == "k":
        body = _k_proj_kernel
        outs = [paged,
                (jax.ShapeDtypeStruct((t, w), BF16), pl.BlockSpec((B, w), lambda i: (i, 0))),
                (jax.ShapeDtypeStruct((nb, 1, w), F32), pl.BlockSpec((None, 1, w), lambda i: (i, 0, 0)))]
    else:
        body = _v_proj_kernel
        outs = [paged,
                (jax.ShapeDtypeStruct((nb, ATT_HEADS, VROWS, B), BF16),
                 pl.BlockSpec((None, ATT_HEADS, VROWS, B), lambda i: (i, 0, 0, 0)))]
    return pl.pallas_call(
        body,
        grid=(nb,),
        in_specs=[pl.BlockSpec((B, k), lambda i: (i, 0)), one, one, one,
                  pl.BlockSpec((k, w), lambda i: (0, 0))],
        out_specs=[o[1] for o in outs],
        out_shape=[o[0] for o in outs],
        compiler_params=_params(("arbitrary",)),
        name=which + "_proj",
    )(x, nw, scale, shift, w_bf16)


def _gated_out_kernel(*refs, has_g, has_final):
    it = iter(refs)
    a_ref = next(it)
    g_ref = next(it) if has_g else None
    x_ref, gate_ref, w_ref = next(it), next(it), next(it)
    fw_ref = next(it) if has_final else None
    o_ref = next(it)
    a = a_ref[...].astype(F32)
    if has_g:
        a = a * _silu(g_ref[...])
    y = x_ref[...] + gate_ref[...] * jnp.dot(a.astype(BF16), w_ref[...], preferred_element_type=F32)
    if has_final:
        ms = jnp.mean(y * y, axis=-1, keepdims=True)
        y = y * lax.rsqrt(ms + EPS) * fw_ref[...]
    o_ref[...] = y


def _gated_out(a, g, x, gate, w_bf16, final_w, *, tm):
    m, k = a.shape
    n = w_bf16.shape[1]
    per_row = gate.shape[0] != 1
    row = lambda width: pl.BlockSpec((tm, width), lambda i: (i, 0))
    one = lambda width: pl.BlockSpec((1, width), lambda i: (0, 0))
    ins, specs = [a], [row(k)]
    if g is not None:
        ins.append(g)
        specs.append(row(k))
    ins += [x, gate, w_bf16]
    specs += [row(n), row(n) if per_row else one(n), pl.BlockSpec((k, n), lambda i: (0, 0))]
    if final_w is not None:
        ins.append(final_w)
        specs.append(one(n))
    return pl.pallas_call(
        functools.partial(_gated_out_kernel, has_g=g is not None, has_final=final_w is not None),
        grid=(m // tm,),
        in_specs=specs,
        out_specs=row(n),
        out_shape=jax.ShapeDtypeStruct((m, n), F32),
        compiler_params=_params(("arbitrary",)),
        name="gated_out",
    )(*ins)


def _ssd_prompt_kernel(x_ref, xbc_ref, z_ref, modv_ref, wdt_ref, wdtT_ref, hrow_ref, hcol_ref,
                       cw_ref, cb_ref, dexp_ref, snw_ref, wout_ref,
                       xo_ref, st_ref, tail_ref,
                       xpad, xg_s, b_s, c_s, xdt_s, abc_s, acst_s, y_s, gn_s):
    L = CHUNK
    ci = pl.program_id(0)
    nci = pl.num_programs(0)

    @pl.when(ci == 0)
    def _():
        xpad[0:8, :] = jnp.zeros((8, CONV_DIM), F32)
        st_ref[...] = jnp.zeros(st_ref.shape, F32)

    x = x_ref[...]
    hn = _norm_mod(x, modv_ref[0:1, :], modv_ref[1:2, :], modv_ref[2:3, :]).astype(BF16)
    dt_raw = jnp.dot(hn, wdt_ref[...], preferred_element_type=F32)
    dt_rawT = lax.dot_general(wdtT_ref[...], hn, NT, preferred_element_type=F32)
    dt = _softplus(dt_raw + hrow_ref[0:1, :])
    dtT = _softplus(dt_rawT + hcol_ref[:, 0:1])
    dA = dt * (-jnp.exp(hrow_ref[1:2, :]))
    dAT = dtT * (-jnp.exp(hcol_ref[:, 1:2]))
    r_i = lax.broadcasted_iota(jnp.int32, (L, L), 0)
    c_i = lax.broadcasted_iota(jnp.int32, (L, L), 1)
    tril = (r_i >= c_i).astype(F32)
    triu = (r_i <= c_i).astype(F32)
    acs = jnp.dot(tril, dA, precision=HI, preferred_element_type=F32)
    acsT = jnp.dot(dAT, triu, precision=HI, preferred_element_type=F32)
    for h in range(SSD_HEADS):
        acst_s[h] = acsT[h:h + 1, :]
        abc_s[h] = jnp.broadcast_to(acsT[h:h + 1, :], (L, 128)).T
    lane = lax.broadcasted_iota(jnp.int32, (L, 128), 1)
    lo_lane = lane < SSD_HEAD_DIM

    def dt_pair(i):
        return jnp.where(lo_lane, jnp.broadcast_to(dt[:, 2 * i:2 * i + 1], (L, 128)),
                         jnp.broadcast_to(dt[:, 2 * i + 1:2 * i + 2], (L, 128)))

    xpad[8:8 + L, :] = xbc_ref[...]
    cw = 512
    for blk in range(CONV_DIM // cw):
        cs = slice(blk * cw, (blk + 1) * cw)
        acc = cb_ref[:, cs] + xpad[5:5 + L, cs] * cw_ref[0:1, cs]
        acc = acc + xpad[6:6 + L, cs] * cw_ref[1:2, cs]
        acc = acc + xpad[7:7 + L, cs] * cw_ref[2:3, cs]
        acc = acc + xpad[8:8 + L, cs] * cw_ref[3:4, cs]
        xc = _silu(acc)
        if blk < 4:
            for q in range(2):
                g = blk * 2 + q
                xg = xc[:, q * 256:(q + 1) * 256]
                xg_s[g] = xg
                xdt_s[2 * g] = xg[:, 0:128] * dt_pair(2 * g)
                xdt_s[2 * g + 1] = xg[:, 128:256] * dt_pair(2 * g + 1)
        elif blk < 6:
            for q in range(4):
                b_s[(blk - 4) * 4 + q] = xc[:, q * 128:(q + 1) * 128].astype(BF16)
        else:
            for q in range(4):
                c_s[(blk - 6) * 4 + q] = xc[:, q * 128:(q + 1) * 128]

    @pl.when(ci == nci - 1)
    def _():
        tail_ref[...] = xpad[5 + L:8 + L, :]

    xpad[5:8, :] = xpad[5 + L:8 + L, :]

    causal = r_i >= c_i
    sub = lax.broadcasted_iota(jnp.int32, (128, 128), 0)
    lo_sub = sub < 64

    def group_compute(g):
        bb = b_s[g]
        cg = c_s[g]
        cbm = lax.dot_general(cg.astype(BF16), bb, NT, preferred_element_type=F32)
        out = []
        for pr in range(2):
            pidx = 2 * g + pr
            xdt_pair = xdt_s[pidx]
            hprev = st_ref[pidx]
            hprev_b = hprev.astype(BF16)
            ypair = jnp.zeros((L, 128), F32)
            dends, cds = [], []
            for hh in range(2):
                h = 2 * pidx + hh
                abc = abc_s[h]
                arow = acst_s[h]
                dec = jnp.exp(jnp.where(causal, abc - arow, NEG))
                mh = (cbm * dec).astype(BF16)
                keep = lo_lane if hh == 0 else jnp.logical_not(lo_lane)
                xm = jnp.where(keep, xdt_pair, 0.0).astype(BF16)
                ypair = ypair + jnp.dot(mh, xm, preferred_element_type=F32)
                cd = (cg * jnp.exp(abc)).astype(BF16)
                yo = lax.dot_general(cd, hprev_b, NT, preferred_element_type=F32)
                ypair = ypair + jnp.where(keep, yo, 0.0)
                tot = abc[L - 1:L, :]
                dends.append(jnp.exp(tot - abc))
                cds.append(jnp.exp(tot))
            dend = jnp.where(lo_lane, dends[0], dends[1])
            xw = (xdt_pair * dend).astype(BF16)
            states = lax.dot_general(xw, bb, TN, preferred_element_type=F32)
            cdm = jnp.where(lo_sub, jnp.broadcast_to(cds[0], (128, 128)),
                            jnp.broadcast_to(cds[1], (128, 128)))
            out.append((pidx, ypair, hprev * cdm + states))
        return out

    def groups_body(i, carry):
        for pidx, ypair, state in group_compute(2 * i) + group_compute(2 * i + 1):
            y_s[pidx] = ypair
            st_ref[pidx] = state
        return carry

    lax.fori_loop(0, SSD_GROUPS // 2, groups_body, 0)

    for g in range(SSD_GROUPS):
        cs = slice(g * 256, (g + 1) * 256)
        yg = jnp.concatenate([y_s[2 * g], y_s[2 * g + 1]], axis=1)
        gg = (yg + xg_s[g] * dexp_ref[:, cs]) * _silu(z_ref[:, cs])
        ms = jnp.mean(gg * gg, axis=-1, keepdims=True)
        gn_s[:, cs] = (gg * lax.rsqrt(ms + EPS) * snw_ref[:, cs]).astype(BF16)

    xo_ref[...] = x + modv_ref[3:4, :] * jnp.dot(gn_s[...], wout_ref[...], preferred_element_type=F32)


def _ssd_prompt(x, proj, modv, wdt, wdtT, hrow, hcol, conv_w, conv_b, dexp, snw, wout):
    t = x.shape[0]
    nc = t // CHUNK
    L = CHUNK
    full = lambda a: pl.BlockSpec(a.shape, lambda c: (0,) * a.ndim)
    return pl.pallas_call(
        _ssd_prompt_kernel,
        grid=(nc,),
        in_specs=[
            pl.BlockSpec((L, D_MODEL), lambda c: (c, 0)),
            pl.BlockSpec((L, CONV_DIM), lambda c: (c, 0)),
            pl.BlockSpec((L, D_INNER), lambda c: (c, 2)),
            full(modv), full(wdt), full(wdtT), full(hrow), full(hcol), full(conv_w), full(conv_b),
            full(dexp), full(snw), full(wout),
        ],
        out_specs=[
            pl.BlockSpec((L, D_MODEL), lambda c: (c, 0)),
            pl.BlockSpec((SSD_HEADS // 2, 128, SSD_STATE), lambda c: (0, 0, 0)),
            pl.BlockSpec((SSD_CONV - 1, CONV_DIM), lambda c: (0, 0)),
        ],
        out_shape=[
            jax.ShapeDtypeStruct((t, D_MODEL), F32),
            jax.ShapeDtypeStruct((SSD_HEADS // 2, 128, SSD_STATE), F32),
            jax.ShapeDtypeStruct((SSD_CONV - 1, CONV_DIM), F32),
        ],
        scratch_shapes=[
            pltpu.VMEM((8 + L, CONV_DIM), F32),
            pltpu.VMEM((SSD_GROUPS, L, 256), F32),
            pltpu.VMEM((SSD_GROUPS, L, SSD_STATE), BF16),
            pltpu.VMEM((SSD_GROUPS, L, SSD_STATE), F32),
            pltpu.VMEM((SSD_HEADS // 2, L, 128), F32),
            pltpu.VMEM((SSD_HEADS, L, 128), F32),
            pltpu.VMEM((SSD_HEADS, 1, L), F32),
            pltpu.VMEM((SSD_HEADS // 2, L, 128), F32),
            pltpu.VMEM((L, D_INNER), BF16),
        ],
        compiler_params=_params(("arbitrary",)),
        name="ssd_prompt",
    )(x, proj, proj, modv, wdt, wdtT, hrow, hcol, conv_w, conv_b, dexp, snw, wout)


def _expansion(heads, width):
    e = np.zeros((heads, heads * width), np.float32)
    for h in range(heads):
        e[h, h * width:(h + 1) * width] = 1.0
    return jnp.asarray(e, BF16)


def _split_mod(mod):
    return mod[..., :D_MODEL], mod[..., D_MODEL:2 * D_MODEL], mod[..., 2 * D_MODEL:]


def _prompt_ssd_layer(x, mod, norm_w, w_in, conv_w, conv_b, dt_bias, a_log, d_skip, ssd_norm_w, w_out):
    shift, scale, gate = _split_mod(mod)
    row = lambda v: v.reshape(1, -1)
    w_main = jnp.concatenate([w_in[:, D_INNER:D_INNER + CONV_DIM], w_in[:, :D_INNER]], axis=1).astype(BF16)
    w_dt = w_in[:, D_INNER + CONV_DIM:].astype(BF16)
    proj = _norm_linear(x, row(norm_w), row(scale), row(shift), w_main, tm=512, tn=2048)
    zeros = jnp.zeros((4, D_MODEL), F32)
    modv = jnp.concatenate([row(norm_w), row(scale), row(shift), row(gate), zeros], axis=0)
    hrow = jnp.concatenate([row(dt_bias), row(a_log), jnp.zeros((6, SSD_HEADS), F32)], axis=0)
    hcol = hrow.T
    dexp = jnp.repeat(d_skip, SSD_HEAD_DIM).reshape(1, D_INNER)
    return _ssd_prompt(x, proj, modv, w_dt, w_dt.T, hrow, hcol, conv_w, row(conv_b), dexp,
                       row(ssd_norm_w), w_out.astype(BF16))


def _topk_block_mask(gate, blk, n_valid):
    nb = gate.shape[0]
    ninf = jnp.float32(-jnp.inf)
    blk_f = blk.astype(F32)
    g = jnp.where(blk < n_valid, gate, ninf)
    sel = jnp.zeros(gate.shape, jnp.bool_)
    for _ in range(MOBA_TOPK):
        m = jnp.max(g, axis=0, keepdims=True)
        first = jnp.min(jnp.where(g == m, blk_f, float(nb)), axis=0, keepdims=True)
        pick = jnp.logical_and(blk_f == first, m > ninf)
        sel = jnp.logical_or(sel, pick)
        g = jnp.where(pick, ninf, g)
    return jnp.where(sel, 0.0, NEG)


def _moba_select_kernel(q_ref, km_ref, o_ref):
    c = pl.program_id(0)
    nb = km_ref.shape[0]
    tq = q_ref.shape[0]
    lane = lax.broadcasted_iota(jnp.int32, (tq, 128), 1)
    blk = lax.broadcasted_iota(jnp.int32, (nb, tq), 0)
    for p in range(ATT_HEADS // 2):
        qp = q_ref[:, p * 128:(p + 1) * 128]
        kmp = km_ref[:, p * 128:(p + 1) * 128]
        for hh in range(2):
            keep = (lane < 64) if hh == 0 else (lane >= 64)
            qm = jnp.where(keep, qp, 0.0)
            gate = _dot3_nt(kmp, qm)
            o_ref[2 * p + hh] = _topk_block_mask(gate, blk, c)


def _moba_select(q, kmean):
    t, w = q.shape
    nb = kmean.shape[0]
    return pl.pallas_call(
        _moba_select_kernel,
        grid=(nb,),
        in_specs=[pl.BlockSpec((MOBA_BLOCK, w), lambda c: (c, 0)),
                  pl.BlockSpec((nb, w), lambda c: (0, 0))],
        out_specs=pl.BlockSpec((ATT_HEADS, nb, MOBA_BLOCK), lambda c: (0, 0, c)),
        out_shape=jax.ShapeDtypeStruct((ATT_HEADS, nb, t), F32),
        compiler_params=_params(("arbitrary",)),
        name="moba_select",
    )(q, kmean)


LOG2E = 1.4426950408889634


def _moba_attn_kernel(slopes_ref, q_ref, k_ref, vh_ref, mask_ref, o_ref, spa_s, spb_s):
    B = MOBA_BLOCK
    hd = ATT_HEAD_DIM
    p = pl.program_id(0)
    c = pl.program_id(1)
    q = q_ref[...] * (hd ** -0.5 * LOG2E)
    lane = lax.broadcasted_iota(jnp.int32, (B, 128), 1)
    qh = [jnp.where(lane < hd, q, 0.0).astype(BF16), jnp.where(lane >= hd, q, 0.0).astype(BF16)]
    s_i = lax.broadcasted_iota(jnp.int32, (B, B), 0)
    t_i = lax.broadcasted_iota(jnp.int32, (B, B), 1)
    dloc = (s_i - t_i).astype(F32)
    sl = [slopes_ref[2 * p] * LOG2E, slopes_ref[2 * p + 1] * LOG2E]
    bh = [dloc * sl[0], dloc * sl[1]]

    def scores(j, hh):
        return lax.dot_general(k_ref[j], qh[hh], NT, preferred_element_type=F32) + bh[hh]

    sps = [jnp.where(s_i <= t_i, scores(c, hh), NEG) for hh in range(2)]
    ms = [jnp.max(sp, axis=0, keepdims=True) for sp in sps]
    pes = [jnp.exp2(sp - m).astype(BF16) for sp, m in zip(sps, ms)]
    accs = [jnp.dot(vh_ref[c, hh], pes[hh], preferred_element_type=F32) for hh in range(2)]

    combos = [(hh, n) for hh in range(2) for n in range(2)]

    def trip_blocks(t):
        return [jnp.minimum(2 * t + n, c) for n in range(2)]

    def put_scores(t, buf):
        js = trip_blocks(t)
        for k, (hh, n) in enumerate(combos):
            buf[k] = scores(js[n], hh)

    def softmax_pv(t, buf, carry):
        m_old, acc_old = carry[:2], carry[2:]
        js = trip_blocks(t)
        radd = {(hh, n): mask_ref[hh, pl.ds(js[n], 1), :] - (c - js[n]).astype(F32) * float(B) * sl[hh]
                for hh, n in combos}
        mx = {cm: jnp.max(buf[k], axis=0, keepdims=True) + radd[cm] for k, cm in enumerate(combos)}
        m_new = [jnp.maximum(m_old[hh], jnp.maximum(mx[(hh, 0)], mx[(hh, 1)])) for hh in range(2)]
        pe = {(hh, n): jnp.exp2(buf[k] - (m_new[hh] - radd[(hh, n)])).astype(BF16)
              for k, (hh, n) in enumerate(combos)}
        pv = {(hh, n): jnp.dot(vh_ref[js[n], hh], pe[(hh, n)], preferred_element_type=F32) for hh, n in combos}
        acc_new = [acc_old[hh] * jnp.exp2(m_old[hh] - m_new[hh]) + (pv[(hh, 0)] + pv[(hh, 1)])
                   for hh in range(2)]
        return tuple(m_new) + tuple(acc_new)

    put_scores(0, spa_s)

    def two_trips(t0, carry):
        put_scores(t0 + 1, spb_s)
        carry = softmax_pv(t0, spa_s, carry)
        put_scores(t0 + 2, spa_s)
        return softmax_pv(t0 + 1, spb_s, carry)

    n_bulk = lax.shift_right_logical(c, 4)
    n_rest = lax.shift_right_logical(c - 16 * n_bulk + 3, 2)

    def bulk(i, carry):
        for u in range(4):
            carry = two_trips(8 * i + 2 * u, carry)
        return carry

    res = lax.fori_loop(0, n_bulk, bulk, tuple(ms) + tuple(accs))
    res = lax.fori_loop(0, n_rest, lambda i, carry: two_trips(8 * n_bulk + 2 * i, carry), res)
    outs = [acc[0:hd, :] / acc[hd:hd + 1, :] for acc in res[2:4]]
    o_ref[...] = jnp.concatenate(outs, axis=0).T


def _moba_attn(q, k_blocks, vh_blocks, mask, slopes):
    t, w = q.shape
    nb = k_blocks.shape[0]
    B = MOBA_BLOCK
    return pl.pallas_call(
        _moba_attn_kernel,
        grid=(ATT_HEADS // 2, nb),
        in_specs=[
            pl.BlockSpec(memory_space=pltpu.SMEM),
            pl.BlockSpec((B, 128), lambda p, c: (c, p)),
            pl.BlockSpec((nb, B, 128), lambda p, c: (0, 0, p)),
            pl.BlockSpec((nb, 2, VROWS, B), lambda p, c: (0, p, 0, 0)),
            pl.BlockSpec((2, nb, B), lambda p, c: (p, 0, c)),
        ],
        out_specs=pl.BlockSpec((B, 128), lambda p, c: (c, p)),
        out_shape=jax.ShapeDtypeStruct((t, w), F32),
        scratch_shapes=[pltpu.VMEM((4, B, B), F32), pltpu.VMEM((4, B, B), F32)],
        compiler_params=_params(("arbitrary", "arbitrary")),
        name="moba_attn",
    )(slopes, q, k_blocks, vh_blocks, mask)


def _alibi_slopes():
    return jnp.asarray(2.0 ** (-8.0 * np.arange(1, ATT_HEADS + 1) / ATT_HEADS), dtype=F32)


def _prompt_moba_layer(x, mod, norm_w, w_in, w_out, final_w):
    shift, scale, gate = _split_mod(mod)
    row = lambda v: v.reshape(1, -1)
    nw, sc, sh = row(norm_w), row(scale), row(shift)
    wq, wk, wv, wg = (w_in[:, i * ATT_WIDTH:(i + 1) * ATT_WIDTH].astype(BF16) for i in range(4))
    B = MOBA_BLOCK
    nb = x.shape[0] // B
    q = _norm_linear(x, nw, sc, sh, wq, tm=512, tn=1024)
    g = _norm_linear(x, nw, sc, sh, wg, tm=512, tn=1024)
    kt, k16, kmean = _kv_proj(x, nw, sc, sh, wk, "k")
    vt, vh16 = _kv_proj(x, nw, sc, sh, wv, "v")
    mask = _moba_select(q, kmean.reshape(nb, ATT_WIDTH))
    o = _moba_attn(q, k16.reshape(nb, B, ATT_WIDTH), vh16, mask, _alibi_slopes())
    y = _gated_out(o, g, x, row(gate), w_out.astype(BF16), row(final_w), tm=512)
    return y, kt, vt


def _ssd_step_kernel(xbc_ref, z_ref, dtr_ref, cst_ref, st_ref, hrow_ref, cw_ref, cb_ref, dexp_ref, snw_ref,
                     e64_ref, sel_ref, gn_ref, sto_ref, cso_ref):
    xbc = xbc_ref[...]
    conv = cb_ref[...] + cst_ref[0:1, :] * cw_ref[0:1, :]
    conv = conv + cst_ref[1:2, :] * cw_ref[1:2, :]
    conv = conv + cst_ref[2:3, :] * cw_ref[2:3, :]
    conv = conv + xbc * cw_ref[3:4, :]
    cso_ref[0:2, :] = cst_ref[1:3, :]
    cso_ref[2:3, :] = xbc
    xc = _silu(conv)
    xs = xc[:, :D_INNER]
    dt = _softplus(dtr_ref[:, :SSD_HEADS] + hrow_ref[0:1, :])
    dec = jnp.exp(dt * (-jnp.exp(hrow_ref[1:2, :])))
    both = _expand_lanes(jnp.concatenate([dt, dec, jnp.zeros((6, SSD_HEADS), F32)], axis=0), e64_ref)
    xdt = xs * both[0:1, :]
    dec_exp = both[1:2, :]
    pieces = [p.astype(F32) for p in _split3(xdt) + _split3(dec_exp)]
    stack = jnp.concatenate(pieces + [jnp.zeros((10, D_INNER), F32)], axis=0).astype(BF16)
    cols = lax.dot_general(stack, sel_ref[...], TN, preferred_element_type=F32)
    ys = []
    for i in range(SSD_HEADS // 2):
        g = i // 2
        brow = xc[:, D_INNER + g * 128:D_INNER + (g + 1) * 128]
        crow = xc[:, D_INNER + GN + g * 128:D_INNER + GN + (g + 1) * 128]
        xcol = cols[i * 128:(i + 1) * 128, 0:1]
        dcol = cols[i * 128:(i + 1) * 128, 1:2]
        hn = st_ref[i] * dcol + xcol * brow
        sto_ref[i] = hn
        c8 = jnp.broadcast_to(crow, (8, SSD_STATE)).astype(BF16)
        ys.append(lax.dot_general(c8, hn.astype(BF16), NT, preferred_element_type=F32)[0:1, :])
    y = jnp.concatenate(ys, axis=1)
    gg = (y + xs * dexp_ref[...]) * _silu(z_ref[...])
    outs = []
    for g in range(SSD_GROUPS):
        gs = gg[:, g * 256:(g + 1) * 256]
        ms = jnp.mean(gs * gs, axis=-1, keepdims=True)
        outs.append(gs * lax.rsqrt(ms + EPS))
    gn_ref[...] = jnp.concatenate(outs, axis=1) * snw_ref[...]


def _ssd_step(proj, dt_raw, conv_state, ssm_state, hrow, conv_w, conv_b, dexp, snw, e64):
    db = proj.shape[0]
    sel = np.zeros((16, 128), np.float32)
    sel[0:3, 0] = 1.0
    sel[3:6, 1] = 1.0
    sel = jnp.asarray(sel, BF16)
    full = lambda a: pl.BlockSpec(a.shape, lambda b: (0,) * a.ndim)
    hp = SSD_HEADS // 2
    return pl.pallas_call(
        _ssd_step_kernel,
        grid=(db,),
        in_specs=[
            pl.BlockSpec((None, 1, CONV_DIM), lambda b: (b, 0, 0)),
            pl.BlockSpec((None, 1, D_INNER), lambda b: (b, 0, 2)),
            pl.BlockSpec((None, 1, 128), lambda b: (b, 0, 0)),
            pl.BlockSpec((None, SSD_CONV - 1, CONV_DIM), lambda b: (b, 0, 0)),
            pl.BlockSpec((None, hp, 128, SSD_STATE), lambda b: (b, 0, 0, 0)),
            full(hrow), full(conv_w), full(conv_b), full(dexp), full(snw), full(e64), full(sel),
        ],
        out_specs=[
            pl.BlockSpec((None, 1, D_INNER), lambda b: (b, 0, 0)),
            pl.BlockSpec((None, hp, 128, SSD_STATE), lambda b: (b, 0, 0, 0)),
            pl.BlockSpec((None, SSD_CONV - 1, CONV_DIM), lambda b: (b, 0, 0)),
        ],
        out_shape=[
            jax.ShapeDtypeStruct((db, 1, D_INNER), F32),
            jax.ShapeDtypeStruct((db, hp, 128, SSD_STATE), F32),
            jax.ShapeDtypeStruct((db, SSD_CONV - 1, CONV_DIM), F32),
        ],
        compiler_params=_params(("arbitrary",)),
        name="ssd_step",
    )(proj, proj, dt_raw, conv_state, ssm_state, hrow, conv_w, conv_b, dexp, snw, e64, sel)


def _sample_ssd_layer(x, mod, norm_w, w_in, conv_w, conv_b, dt_bias, a_log, d_skip, ssd_norm_w, w_out,
                      conv_state, ssm_state):
    db = x.shape[0]
    shift, scale, gate = _split_mod(mod)
    row = lambda v: v.reshape(1, -1)
    w_main = jnp.concatenate([w_in[:, D_INNER:D_INNER + CONV_DIM], w_in[:, :D_INNER]], axis=1).astype(BF16)
    w_dt = jnp.pad(w_in[:, D_INNER + CONV_DIM:], ((0, 0), (0, 128 - SSD_HEADS))).astype(BF16)
    proj = _norm_linear(x, row(norm_w), scale, shift, w_main, tm=db, tn=2048)
    dt_raw = _norm_linear(x, row(norm_w), scale, shift, w_dt, tm=db, tn=128)
    hrow = jnp.concatenate([row(dt_bias), row(a_log), jnp.zeros((6, SSD_HEADS), F32)], axis=0)
    dexp = jnp.repeat(d_skip, SSD_HEAD_DIM).reshape(1, D_INNER)
    gn, st, cs = _ssd_step(proj.reshape(db, 1, -1), dt_raw.reshape(db, 1, 128), conv_state,
                           ssm_state.reshape(db, SSD_HEADS // 2, 128, SSD_STATE), hrow, conv_w, row(conv_b),
                           dexp, row(ssd_norm_w), _expansion(SSD_HEADS, SSD_HEAD_DIM))
    x1 = _gated_out(gn.reshape(db, D_INNER), None, x, gate, w_out.astype(BF16), None, tm=db)
    return x1, st, cs


PAGES_PER_STEP = 16
PAGES_PER_BLOCK = MOBA_BLOCK // PAGE


def _paged_mean_kernel(pt_ref, *refs):
    pages, o_ref = refs[:PAGES_PER_STEP], refs[PAGES_PER_STEP]
    s = pl.program_id(1)
    w, nb = o_ref.shape
    bps = PAGES_PER_STEP // PAGES_PER_BLOCK

    @pl.when(s == 0)
    def _():
        o_ref[...] = jnp.zeros((w, nb), F32)

    lane = lax.broadcasted_iota(jnp.int32, (w, nb), 1)
    cur = o_ref[...]
    for r in range(bps):
        tot = pages[2 * r][...].reshape(w, PAGE) + pages[2 * r + 1][...].reshape(w, PAGE)
        col = jnp.sum(tot, axis=-1, keepdims=True) * (1.0 / MOBA_BLOCK)
        cur = jnp.where(lane == s * bps + r, col, cur)
    o_ref[...] = cur


def _paged_mean(cache_t, page_table):
    db, n_pages = page_table.shape
    nsteps = n_pages // PAGES_PER_STEP
    nblk = n_pages // PAGES_PER_BLOCK

    def page_spec(r):
        return pl.BlockSpec((None, ATT_HEADS, ATT_HEAD_DIM, PAGE),
                            lambda b, s, pt: (pt[b, s * PAGES_PER_STEP + r], 0, 0, 0))

    return pl.pallas_call(
        _paged_mean_kernel,
        grid_spec=pltpu.PrefetchScalarGridSpec(
            num_scalar_prefetch=1,
            grid=(db, nsteps),
            in_specs=[page_spec(r) for r in range(PAGES_PER_STEP)],
            out_specs=pl.BlockSpec((None, ATT_WIDTH, nblk), lambda b, s, pt: (b, 0, 0)),
        ),
        out_shape=jax.ShapeDtypeStruct((db, ATT_WIDTH, nblk), F32),
        compiler_params=_params(("arbitrary", "arbitrary")),
        name="paged_mean",
    )(page_table, *([cache_t] * PAGES_PER_STEP))


def _sample_select_kernel(q_ref, km_ref, o_ref):
    w, nb = km_ref.shape
    prod = km_ref[...] * q_ref[...]
    gate = jnp.sum(prod.reshape(ATT_HEADS, ATT_HEAD_DIM, nb), axis=1)
    blk_f = lax.broadcasted_iota(jnp.int32, gate.shape, 1).astype(F32)
    lane = lax.broadcasted_iota(jnp.int32, (ATT_HEADS, 128), 1)
    ninf = jnp.float32(-jnp.inf)
    g = gate
    out = jnp.zeros((ATT_HEADS, 128), F32)
    for r in range(MOBA_TOPK):
        m = jnp.max(g, axis=-1, keepdims=True)
        first = jnp.min(jnp.where(g == m, blk_f, float(nb)), axis=-1, keepdims=True)
        out = jnp.where(lane == r, first, out)
        g = jnp.where(blk_f == first, ninf, g)
    o_ref[...] = out.astype(jnp.int32)


def _sample_select(q_col, kmean_t):
    db, w, nb = kmean_t.shape
    return pl.pallas_call(
        _sample_select_kernel,
        grid=(db,),
        in_specs=[pl.BlockSpec((None, w, 1), lambda b: (b, 0, 0)),
                  pl.BlockSpec((None, w, nb), lambda b: (b, 0, 0))],
        out_specs=pl.BlockSpec((None, ATT_HEADS, 128), lambda b: (b, 0, 0)),
        out_shape=jax.ShapeDtypeStruct((db, ATT_HEADS, 128), jnp.int32),
        compiler_params=_params(("arbitrary",)),
        name="sample_select",
    )(q_col, kmean_t)


def _sel_copies(pt_ref, idx_ref, cache_hbm, buf, sem, b):
    out = []
    for h in range(ATT_HEADS):
        for r in range(MOBA_TOPK):
            blk = idx_ref[b, h * MOBA_TOPK + r]
            for half in range(PAGES_PER_BLOCK):
                page = pt_ref[b, blk * PAGES_PER_BLOCK + half]
                out.append(pltpu.make_async_copy(cache_hbm.at[page, h],
                                                 buf.at[h, r * PAGES_PER_BLOCK + half], sem))
    return out


def _sample_attn_kernel(pt_ref, idx_ref, q_ref, kn_ref, vn_ref, idxv_ref, slopes_ref, ck_hbm, cv_hbm, o_ref,
                        kbuf, vbuf, sems, *, past_len):
    b = pl.program_id(0)
    kcopies = _sel_copies(pt_ref, idx_ref, ck_hbm, kbuf, sems.at[0], b)
    vcopies = _sel_copies(pt_ref, idx_ref, cv_hbm, vbuf, sems.at[1], b)
    for cp in kcopies + vcopies:
        cp.start()
    for cp in kcopies + vcopies:
        cp.wait()
    nsel = MOBA_TOPK * MOBA_BLOCK
    npg = MOBA_TOPK * PAGES_PER_BLOCK
    scale = ATT_HEAD_DIM ** -0.5
    q = q_ref[...]
    rows = []
    for h in range(ATT_HEADS):
        q8 = jnp.broadcast_to(q[h:h + 1, :], (8, ATT_HEAD_DIM)).astype(BF16)
        kcat = jnp.concatenate([kbuf[h, i] for i in range(npg)], axis=1).astype(BF16)
        rows.append(jnp.dot(q8, kcat, preferred_element_type=F32)[0:1, :])
    s = jnp.concatenate(rows, axis=0) * scale
    local = lax.broadcasted_iota(jnp.int32, (ATT_HEADS, nsel), 1)
    within = jnp.bitwise_and(local, MOBA_BLOCK - 1).astype(F32)
    which = jnp.right_shift(local, MOBA_BLOCK.bit_length() - 1)
    starts = (idxv_ref[...] * MOBA_BLOCK).astype(F32)
    start = jnp.zeros((ATT_HEADS, nsel), F32)
    for r in range(MOBA_TOPK):
        start = jnp.where(which == r, starts[:, r:r + 1], start)
    slope = slopes_ref[:, 0:1]
    s = s - slope * (float(past_len) - (start + within))
    s_self = jnp.sum(q.astype(BF16).astype(F32) * kn_ref[...].astype(BF16).astype(F32),
                     axis=-1, keepdims=True) * scale
    m = jnp.maximum(jnp.max(s, axis=-1, keepdims=True), s_self)
    pe = jnp.exp(s - m)
    p_self = jnp.exp(s_self - m)
    l = jnp.sum(pe, axis=-1, keepdims=True) + p_self
    pe16 = pe.astype(BF16)
    rows = []
    for h in range(ATT_HEADS):
        vcat = jnp.concatenate([vbuf[h, i] for i in range(npg)], axis=1).astype(BF16)
        p8 = jnp.broadcast_to(pe16[h:h + 1, :], (8, nsel))
        rows.append(lax.dot_general(p8, vcat, NT, preferred_element_type=F32)[0:1, :])
    o_ref[...] = (jnp.concatenate(rows, axis=0) + p_self * vn_ref[...]) / l


def _sample_attn(q, k_new, v_new, cache_kt, cache_vt, page_table, idx, idx_lanes, slopes):
    db, n_pages = page_table.shape
    npg = MOBA_TOPK * PAGES_PER_BLOCK
    vec = pl.BlockSpec((None, ATT_HEADS, ATT_HEAD_DIM), lambda b, pt, ix: (b, 0, 0))
    slopes = jnp.broadcast_to(slopes[:, None], (ATT_HEADS, 128))
    return pl.pallas_call(
        functools.partial(_sample_attn_kernel, past_len=n_pages * PAGE),
        grid_spec=pltpu.PrefetchScalarGridSpec(
            num_scalar_prefetch=2,
            grid=(db,),
            in_specs=[vec, vec, vec,
                      pl.BlockSpec((None, ATT_HEADS, 128), lambda b, pt, ix: (b, 0, 0)),
                      pl.BlockSpec((ATT_HEADS, 128), lambda b, pt, ix: (0, 0)),
                      pl.BlockSpec(memory_space=pl.ANY), pl.BlockSpec(memory_space=pl.ANY)],
            out_specs=vec,
            scratch_shapes=[pltpu.VMEM((ATT_HEADS, npg, ATT_HEAD_DIM, PAGE), F32),
                            pltpu.VMEM((ATT_HEADS, npg, ATT_HEAD_DIM, PAGE), F32),
                            pltpu.SemaphoreType.DMA((2,))],
        ),
        out_shape=jax.ShapeDtypeStruct((db, ATT_HEADS, ATT_HEAD_DIM), F32),
        compiler_params=_params(("arbitrary",)),
        name="sample_attn",
    )(page_table, idx, q, k_new, v_new, idx_lanes, slopes, cache_kt, cache_vt)


def _sample_moba_layer(x, mod, norm_w, w_in, w_out, final_w, cache_kt, cache_vt, page_table):
    db = x.shape[0]
    shift, scale, gate = _split_mod(mod)
    row = lambda v: v.reshape(1, -1)
    proj = _norm_linear(x, row(norm_w), scale, shift, w_in.astype(BF16), tm=db, tn=1024)
    hs = (db, ATT_HEADS, ATT_HEAD_DIM)
    q, k, v = (proj[:, i * ATT_WIDTH:(i + 1) * ATT_WIDTH].reshape(hs) for i in range(3))
    g = proj[:, 3 * ATT_WIDTH:]
    idx_lanes = _sample_select(q.reshape(db, ATT_WIDTH, 1), _paged_mean(cache_kt, page_table))
    idx = idx_lanes[:, :, :MOBA_TOPK].reshape(db, ATT_HEADS * MOBA_TOPK)
    o = _sample_attn(q, k, v, cache_kt, cache_vt, page_table, idx, idx_lanes, _alibi_slopes())
    y = _gated_out(o.reshape(db, ATT_WIDTH), g, x, gate, w_out.astype(BF16), row(final_w), tm=db)
    return y, k, v


def kernel(x_prompt, x_sample, cache_k, cache_v, state_ssm, state_conv, page_table, c_prompt, c_sample,
           ada_w, ada_b, norm_w, ssd_w_in, ssd_conv_w, ssd_conv_b, ssd_dt_bias, ssd_a_log, ssd_d,
           ssd_norm_w, ssd_w_out, att_w_in, att_w_out, final_norm_w):
    bsz, t, d = x_prompt.shape
    db = x_sample.shape[0]
    c_rows = jnp.concatenate([c_prompt, c_sample, jnp.zeros((7, d), F32)], axis=0)
    mod = _ada_mod(c_rows, ada_w, ada_b)

    xp = x_prompt[0]
    xp, ssm_p, conv_p = _prompt_ssd_layer(xp, mod[0, 0], norm_w[0], ssd_w_in[0], ssd_conv_w[0], ssd_conv_b[0],
                                          ssd_dt_bias[0], ssd_a_log[0], ssd_d[0], ssd_norm_w[0], ssd_w_out[0])
    y_p, k_p, v_p = _prompt_moba_layer(xp, mod[1, 0], norm_w[1], att_w_in[0], att_w_out[0], final_norm_w)

    xs = x_sample[:, 0]
    xs, ssm_s, conv_s = _sample_ssd_layer(xs, mod[0, 1:1 + db], norm_w[0], ssd_w_in[0], ssd_conv_w[0], ssd_conv_b[0],
                                          ssd_dt_bias[0], ssd_a_log[0], ssd_d[0], ssd_norm_w[0], ssd_w_out[0],
                                          state_conv[0], state_ssm[0])
    y_s, k_s, v_s = _sample_moba_layer(xs, mod[1, 1:1 + db], norm_w[1], att_w_in[0], att_w_out[0], final_norm_w,
                                       jnp.transpose(cache_k[0], (0, 2, 3, 1)),
                                       jnp.transpose(cache_v[0], (0, 2, 3, 1)), page_table)

    n_pg = t // PAGE
    kv_s_shape = (1, db, 1, ATT_HEADS, ATT_HEAD_DIM)

    def paged_kv(a):
        a = a.reshape(n_pg, ATT_HEADS, ATT_HEAD_DIM, PAGE)
        return jnp.transpose(a, (0, 3, 1, 2)).reshape(1, bsz, n_pg, PAGE, ATT_HEADS, ATT_HEAD_DIM)

    return (y_p.reshape(bsz, t, d),
            y_s.reshape(db, 1, d),
            paged_kv(k_p), paged_kv(v_p),
            k_s.reshape(kv_s_shape), v_s.reshape(kv_s_shape),
            ssm_p.reshape(1, bsz, SSD_HEADS, SSD_HEAD_DIM, SSD_STATE),
            conv_p.reshape(1, bsz, SSD_CONV - 1, CONV_DIM),
            ssm_s.reshape(1, db, SSD_HEADS, SSD_HEAD_DIM, SSD_STATE),
            conv_s.reshape(1, db, SSD_CONV - 1, CONV_DIM))
```
